```python
import math
import jax, jax.numpy as jnp
from jax import lax
import numpy as np

D_MODEL = 1024
BATCH = 2
SEQ = 8192
DEPTH = 4

N_MIXERS = 2
N_LAYERS_A = (DEPTH + 1) // 2
N_LAYERS_B = DEPTH // 2

MEM_LEN = 256
MEM_HEADS = 4
MEM_HEAD_DIM = 64
MEM_WIDTH = MEM_HEADS * MEM_HEAD_DIM
TOK_WIDTH = D_MODEL - MEM_WIDTH

RNN_WIDTH = TOK_WIDTH
RNN_BLOCKS = 8
RNN_BLOCK_DIM = RNN_WIDTH // RNN_BLOCKS
CONV_WIDTH = 4
LRU_C = 8.0

DIFF_HEADS = 6
DIFF_QK_DIM = 64
DIFF_V_DIM = 2 * DIFF_QK_DIM
DIFF_QK_WIDTH = DIFF_HEADS * 2 * DIFF_QK_DIM
DIFF_V_WIDTH = DIFF_HEADS * DIFF_V_DIM
Q_BLOCK = 128

REL_BUCKETS = 32
REL_MAX_EXACT = 16
REL_MAX_DIST = 128

N_GROUPS = 4
EXPERTS_PER_GROUP = 4
N_EXPERTS = N_GROUPS * EXPERTS_PER_GROUP
TOP_K_INNER = 2
D_EXPERT = D_MODEL // 4

IN_A = 2 * RNN_WIDTH + MEM_WIDTH
IN_B = 2 * DIFF_QK_WIDTH + DIFF_V_WIDTH + MEM_WIDTH
EPS = 1e-6

kernel_name = "hybrid_rglru_diffattn_hiermoe"


def rmsnorm(x, g):
    x32 = x.astype(jnp.float32)
    y = x32 * lax.rsqrt(jnp.mean(x32 * x32, axis=-1, keepdims=True) + EPS) * g.astype(jnp.float32)
    return y.astype(x.dtype)


def t5_bucket(q_pos, k_pos):
    n = jnp.maximum(q_pos[:, None] - k_pos[None, :], 0)
    nf = jnp.maximum(n, 1).astype(jnp.float32)
    large = REL_MAX_EXACT + (jnp.log(nf / REL_MAX_EXACT) / math.log(REL_MAX_DIST / REL_MAX_EXACT)
                             * (REL_BUCKETS - REL_MAX_EXACT)).astype(jnp.int32)
    large = jnp.minimum(large, REL_BUCKETS - 1)
    return jnp.where(n < REL_MAX_EXACT, n, large)


def _lin_combine(c1, c2):
    a1, b1 = c1
    a2, b2 = c2
    return a1 * a2, a2 * b1 + b2


def rglru_mixer(u_x, u_gate, conv_w, conv_b, wa, ba, wx, bx, lam):
    B, S, C = u_x.shape
    xc = lax.conv_general_dilated(
        u_x, conv_w[:, None, :], window_strides=(1,), padding=[(CONV_WIDTH - 1, 0)],
        dimension_numbers=('NWC', 'WIO', 'NWC'), feature_group_count=C) + conv_b
    xb = xc.reshape(B, S, RNN_BLOCKS, RNN_BLOCK_DIM)
    r = jax.nn.sigmoid((jnp.einsum('bsgi,gij->bsgj', xb, wa).reshape(B, S, C) + ba).astype(jnp.float32))
    ig = jax.nn.sigmoid((jnp.einsum('bsgi,gij->bsgj', xb, wx).reshape(B, S, C) + bx).astype(jnp.float32))
    log_a = -LRU_C * r * jax.nn.softplus(-lam.astype(jnp.float32))
    a = jnp.exp(log_a)
    b = jnp.sqrt(-jnp.expm1(2.0 * log_a)) * (ig * xc.astype(jnp.float32))
    _, h = lax.associative_scan(_lin_combine, (a, b), axis=1)
    return (h * jax.nn.gelu(u_gate.astype(jnp.float32))).astype(u_x.dtype)


def diff_attention(q, k, v, lq1, lk1, lq2, lk2, subln_g, rel_bias, lambda_init):
    B, S, _ = q.shape
    qh = q.reshape(B, S, DIFF_HEADS, 2, DIFF_QK_DIM).transpose(3, 0, 2, 1, 4)
    kh = k.reshape(B, S, DIFF_HEADS, 2, DIFF_QK_DIM).transpose(3, 0, 2, 1, 4)
    vh = v.reshape(B, S, DIFF_HEADS, DIFF_V_DIM).transpose(0, 2, 1, 3)
    f32 = jnp.float32
    lam = (jnp.exp(jnp.sum(lq1.astype(f32) * lk1.astype(f32)))
           - jnp.exp(jnp.sum(lq2.astype(f32) * lk2.astype(f32))) + lambda_init)
    scale = DIFF_QK_DIM ** -0.5
    outs = []
    for blk in range(S // Q_BLOCK):
        start = blk * Q_BLOCK
        end = start + Q_BLOCK
        qb = qh[:, :, :, start:end]
        kb = kh[:, :, :, :end]
        s = jnp.einsum('nbhqd,nbhkd->nbhqk', qb, kb).astype(f32) * scale
        q_pos = jnp.arange(start, end)
        k_pos = jnp.arange(end)
        bias = rel_bias[t5_bucket(q_pos, k_pos)].astype(f32).transpose(2, 0, 1)
        s = jnp.where(k_pos[None, :] <= q_pos[:, None], s + bias, -jnp.inf)
        p = jax.nn.softmax(s, axis=-1)
        o = jnp.einsum('bhqk,bhkd->bhqd', p[0] - lam * p[1], vh[:, :, :end].astype(f32))
        outs.append(o)
    o = jnp.concatenate(outs, axis=2)
    o = rmsnorm(o, subln_g) * (1.0 - lambda_init)
    return o.transpose(0, 2, 1, 3).reshape(B, S, DIFF_V_WIDTH).astype(q.dtype)


def memory_attention(q_mem, mem_k, mem_v):
    B, S, _ = q_mem.shape
    qh = q_mem.reshape(B, S, MEM_HEADS, MEM_HEAD_DIM)
    s = jnp.einsum('bshd,bmhd->bhsm', qh, mem_k).astype(jnp.float32) * (MEM_HEAD_DIM ** -0.5)
    p = jax.nn.softmax(s, axis=-1)
    o = jnp.einsum('bhsm,bmhd->bshd', p, mem_v.astype(jnp.float32))
    return o.reshape(B, S, MEM_WIDTH).astype(q_mem.dtype)


def hier_moe(h, w_group, w_router, w1, w3, w2):
    B, S, D = h.shape
    t = h.reshape(B * S, D)
    g_logits = (t @ w_group).astype(jnp.float32)
    g_prob = jax.nn.softmax(g_logits, axis=-1)
    g_idx = jnp.argmax(g_logits, axis=-1)
    g_w = jnp.take_along_axis(g_prob, g_idx[:, None], axis=1)
    e_logits = (t @ w_router).astype(jnp.float32).reshape(-1, N_GROUPS, EXPERTS_PER_GROUP)
    e_in = jnp.take_along_axis(e_logits, g_idx[:, None, None], axis=1)[:, 0]
    top_v, top_i = lax.top_k(e_in, TOP_K_INNER)
    w = jax.nn.softmax(top_v, axis=-1) * g_w
    eid = g_idx[:, None] * EXPERTS_PER_GROUP + top_i
    gates = jnp.sum(jax.nn.one_hot(eid, N_EXPERTS, dtype=jnp.float32) * w[..., None], axis=1)
    a = jnp.einsum('td,edf->tef', t, w1)
    u = jnp.einsum('td,edf->tef', t, w3)
    hid = jax.nn.silu(a) * u * gates[:, :, None].astype(t.dtype)
    y = jnp.einsum('tef,efd->td', hid, w2)
    return y.reshape(B, S, D)


def setup_inputs(seed: int = 0) -> dict:
    key = jax.random.key(seed)
    ks = jax.random.split(key, 32)
    nrm = jax.random.normal
    f32 = jnp.float32
    u = jax.random.uniform(ks[12], (N_LAYERS_A, RNN_WIDTH), f32, minval=0.9, maxval=0.999)
    a0 = u ** (1.0 / LRU_C)
    lru_lambda = jnp.log(a0) - jnp.log1p(-a0)
    return {
        "x": nrm(ks[0], (BATCH, SEQ, D_MODEL), f32),
        "mem": nrm(ks[1], (BATCH, MEM_LEN, D_MODEL), f32),
        "norm_mix_g": 1.0 + 0.05 * nrm(ks[2], (DEPTH, D_MODEL), f32),
        "w_in_a": nrm(ks[3], (N_LAYERS_A, D_MODEL, IN_A), f32) * D_MODEL ** -0.5,
        "w_in_b": nrm(ks[4], (N_LAYERS_B, D_MODEL, IN_B), f32) * D_MODEL ** -0.5,
        "w_out": nrm(ks[5], (DEPTH, D_MODEL, D_MODEL), f32) * D_MODEL ** -0.5,
        "conv_w": nrm(ks[6], (N_LAYERS_A, CONV_WIDTH, RNN_WIDTH), f32) * CONV_WIDTH ** -0.5,
        "conv_b": 0.01 * nrm(ks[7], (N_LAYERS_A, RNN_WIDTH), f32),
        "gate_a_w": nrm(ks[8], (N_LAYERS_A, RNN_BLOCKS, RNN_BLOCK_DIM, RNN_BLOCK_DIM), f32) * RNN_BLOCK_DIM ** -0.5,
        "gate_a_b": 0.01 * nrm(ks[9], (N_LAYERS_A, RNN_WIDTH), f32),
        "gate_x_w": nrm(ks[10], (N_LAYERS_A, RNN_BLOCKS, RNN_BLOCK_DIM, RNN_BLOCK_DIM), f32) * RNN_BLOCK_DIM ** -0.5,
        "gate_x_b": 0.01 * nrm(ks[11], (N_LAYERS_A, RNN_WIDTH), f32),
        "lru_lambda": lru_lambda,
        "diff_lq1": 0.1 * nrm(ks[13], (N_LAYERS_B, DIFF_QK_DIM), f32),
        "diff_lk1": 0.1 * nrm(ks[14], (N_LAYERS_B, DIFF_QK_DIM), f32),
        "diff_lq2": 0.1 * nrm(ks[15], (N_LAYERS_B, DIFF_QK_DIM), f32),
        "diff_lk2": 0.1 * nrm(ks[16], (N_LAYERS_B, DIFF_QK_DIM), f32),
        "diff_subln_g": 1.0 + 0.05 * nrm(ks[17], (N_LAYERS_B, DIFF_V_DIM), f32),
        "rel_bias": 0.5 * nrm(ks[18], (REL_BUCKETS, DIFF_HEADS), f32),
        "mem_norm_g": 1.0 + 0.05 * nrm(ks[19], (D_MODEL,), f32),
        "w_mem_kv": nrm(ks[20], (DEPTH, D_MODEL, 2 * MEM_WIDTH), f32) * D_MODEL ** -0.5,
        "norm_ffn_g": 1.0 + 0.05 * nrm(ks[21], (DEPTH, D_MODEL), f32),
        "w_group": nrm(ks[22], (DEPTH, D_MODEL, N_GROUPS), f32) * D_MODEL ** -0.5,
        "w_router": nrm(ks[23], (DEPTH, D_MODEL, N_EXPERTS), f32) * D_MODEL ** -0.5,
        "w_exp1": nrm(ks[24], (DEPTH, N_EXPERTS, D_MODEL, D_EXPERT), f32) * D_MODEL ** -0.5,
        "w_exp3": nrm(ks[25], (DEPTH, N_EXPERTS, D_MODEL, D_EXPERT), f32) * D_MODEL ** -0.5,
        "w_exp2": nrm(ks[26], (DEPTH, N_EXPERTS, D_EXPERT, D_MODEL), f32) * D_EXPERT ** -0.5,
        "final_norm_g": 1.0 + 0.05 * nrm(ks[27], (D_MODEL,), f32),
    }


def reference(x, mem, norm_mix_g, w_in_a, w_in_b, w_out, conv_w, conv_b, gate_a_w, gate_a_b,
              gate_x_w, gate_x_b, lru_lambda, diff_lq1, diff_lk1, diff_lq2, diff_lk2, diff_subln_g,
              rel_bias, mem_norm_g, w_mem_kv, norm_ffn_g, w_group, w_router, w_exp1, w_exp3,
              w_exp2, final_norm_g):
    B = x.shape[0]
    mem_n = rmsnorm(mem, mem_norm_g)
    h = x
    for i in range(DEPTH):
        j = i // N_MIXERS
        hn = rmsnorm(h, norm_mix_g[i])
        kv = (mem_n @ w_mem_kv[i]).reshape(B, MEM_LEN, 2, MEM_HEADS, MEM_HEAD_DIM)
        if i % N_MIXERS == 0:
            proj = hn @ w_in_a[j]
            u_x, u_g, q_mem = jnp.split(proj, [RNN_WIDTH, 2 * RNN_WIDTH], axis=-1)
            tok = rglru_mixer(u_x, u_g, conv_w[j], conv_b[j], gate_a_w[j], gate_a_b[j],
                              gate_x_w[j], gate_x_b[j], lru_lambda[j])
        else:
            proj = hn @ w_in_b[j]
            q, k, v, q_mem = jnp.split(
                proj, [DIFF_QK_WIDTH, 2 * DIFF_QK_WIDTH, 2 * DIFF_QK_WIDTH + DIFF_V_WIDTH], axis=-1)
            lambda_init = 0.8 - 0.6 * math.exp(-0.3 * i)
            tok = diff_attention(q, k, v, diff_lq1[j], diff_lk1[j], diff_lq2[j], diff_lk2[j],
                                 diff_subln_g[j], rel_bias, lambda_init)
        mo = memory_attention(q_mem, kv[:, :, 0], kv[:, :, 1])
        h = h + jnp.concatenate([tok, mo], axis=-1) @ w_out[i]
        h = h + hier_moe(rmsnorm(h, norm_ffn_g[i]), w_group[i], w_router[i],
                         w_exp1[i], w_exp3[i], w_exp2[i])
    return rmsnorm(h, final_norm_g)
```

```python
import functools
import math

import numpy as np
import jax
import jax.numpy as jnp
from jax import lax
from jax.experimental import pallas as pl
from jax.experimental.pallas import tpu as pltpu

F32 = jnp.float32
BF16 = jnp.bfloat16
EPS = 1e-6
NEG = -1e30

MEM_HEADS = 4
MEM_HEAD_DIM = 64
MEM_WIDTH = MEM_HEADS * MEM_HEAD_DIM
CONV_WIDTH = 4
LRU_C = 8.0
DIFF_QK_DIM = 64
DIFF_V_DIM = 2 * DIFF_QK_DIM
REL_BUCKETS = 32
REL_MAX_EXACT = 16
REL_MAX_DIST = 128
N_GROUPS = 4
EXPERTS_PER_GROUP = 4
LANES = 128


def _rms(x, g):
    return x * lax.rsqrt(jnp.mean(x * x, axis=-1, keepdims=True) + EPS) * g


def _dot(a, b):
    return jnp.dot(a, b, preferred_element_type=F32)


def _dot_nt(a, b):
    return lax.dot_general(a, b, (((1,), (1,)), ((), ())), preferred_element_type=F32)


def _memkv_kernel(mem_ref, g_ref, w_ref, o_ref):
    xn = _rms(mem_ref[...], g_ref[...]).astype(BF16)
    o_ref[0] = _dot(xn, w_ref[0]).astype(BF16)


def _memkv(mem2d, g, w_bf16):
    depth, d, n = w_bf16.shape
    rows = mem2d.shape[0]
    return pl.pallas_call(
        _memkv_kernel,
        grid=(depth,),
        in_specs=[pl.BlockSpec((rows, d), lambda i: (0, 0)),
                  pl.BlockSpec((1, d), lambda i: (0, 0)),
                  pl.BlockSpec((1, d, n), lambda i: (i, 0, 0))],
        out_specs=pl.BlockSpec((1, rows, n), lambda i: (i, 0, 0)),
        out_shape=jax.ShapeDtypeStruct((depth, rows, n), BF16),
        name="memkv",
    )(mem2d, g, w_bf16)


def _mem_attn(qm, kv):
    outs = []
    for hh in range(MEM_HEADS):
        lo, hi = hh * MEM_HEAD_DIM, (hh + 1) * MEM_HEAD_DIM
        q = (qm[:, lo:hi] * (MEM_HEAD_DIM ** -0.5)).astype(BF16)
        k = kv[:, lo:hi]
        v = kv[:, MEM_WIDTH + lo:MEM_WIDTH + hi]
        s = _dot_nt(q, k)
        m = jnp.max(s, axis=-1, keepdims=True)
        p = jnp.exp(s - m)
        l = jnp.sum(p, axis=-1, keepdims=True)
        outs.append(_dot(p.astype(BF16), v) / l)
    return jnp.concatenate(outs, axis=-1)


def _in_kernel(h_ref, g_ref, w_ref, kv_ref, *out_refs, splits, scales):
    hn = _rms(h_ref[...], g_ref[...]).astype(BF16)
    proj = _dot(hn, w_ref[...])
    off = 0
    for ref, width, scale in zip(out_refs[:-1], splits, scales):
        piece = proj[:, off:off + width]
        if scale != 1.0:
            piece = piece * scale
        ref[...] = piece.astype(ref.dtype)
        off += width
    out_refs[-1][...] = _mem_attn(proj[:, off:off + MEM_WIDTH], kv_ref[0, 0]).astype(BF16)


def _in_proj(h, g, w_bf16, kv, layer, batch, splits, scales, dtypes, tm):
    t, d = h.shape
    n = w_bf16.shape[1]
    tiles_per_batch = (t // batch) // tm
    mem_len = kv.shape[2]
    out_shape = [jax.ShapeDtypeStruct((t, w), dt) for w, dt in zip(splits, dtypes)]
    out_shape.append(jax.ShapeDtypeStruct((t, MEM_WIDTH), BF16))
    out_specs = [pl.BlockSpec((tm, w), lambda i: (i, 0)) for w in splits]
    out_specs.append(pl.BlockSpec((tm, MEM_WIDTH), lambda i: (i, 0)))
    return pl.pallas_call(
        functools.partial(_in_kernel, splits=splits, scales=scales),
        grid=(t // tm,),
        in_specs=[pl.BlockSpec((tm, d), lambda i: (i, 0)),
                  pl.BlockSpec((1, d), lambda i: (0, 0)),
                  pl.BlockSpec((d, n), lambda i: (0, 0)),
                  pl.BlockSpec((1, 1, mem_len, 2 * MEM_WIDTH),
                               lambda i: (layer, i // tiles_per_batch, 0, 0))],
        out_specs=out_specs,
        out_shape=out_shape,
        name="in_proj",
    )(h, g, w_bf16, kv)


def _gelu_tanh(x):
    return 0.5 * x * (1.0 + jnp.tanh(math.sqrt(2.0 / math.pi) * (x + 0.044715 * (x * x * x))))


def _softplus(z):
    return jnp.maximum(z, 0.0) + jnp.log1p(jnp.exp(-jnp.abs(z)))


def _rglru_kernel(ux_ref, ug_ref, cw_ref, cb_ref, wa_ref, ba_ref, wx_ref, bx_ref, lam_ref,
                  o_ref, xpad_sc, h_sc):
    ts, c = ux_ref.shape
    pad = 8

    @pl.when(pl.program_id(1) == 0)
    def _():
        xpad_sc[0:pad, :] = jnp.zeros((pad, c), F32)
        h_sc[...] = jnp.zeros_like(h_sc)

    ux = ux_ref[...]
    xpad_sc[pad:pad + ts, :] = ux
    xc = cb_ref[...] + cw_ref[CONV_WIDTH - 1:CONV_WIDTH, :] * ux
    for j in range(CONV_WIDTH - 1):
        xc = xc + cw_ref[j:j + 1, :] * xpad_sc[pl.ds(pad - (CONV_WIDTH - 1) + j, ts), :]
    xpad_sc[0:pad, :] = ux[ts - pad:ts, :]

    xcb = xc.astype(BF16)
    r = jax.nn.sigmoid(_dot(xcb, wa_ref[...]) + ba_ref[...])
    ig = jax.nn.sigmoid(_dot(xcb, wx_ref[...]) + bx_ref[...])
    log_a = (-LRU_C) * r * _softplus(-lam_ref[...])
    a = jnp.exp(log_a)
    b = jnp.sqrt(-jnp.tanh(log_a) * (a * a + 1.0)) * (ig * xc)

    row = lax.broadcasted_iota(jnp.int32, (ts, c), 0)
    k = 1
    while k < ts:
        keep = row >= k
        a_s = jnp.where(keep, pltpu.roll(a, k, 0), 1.0)
        b_s = jnp.where(keep, pltpu.roll(b, k, 0), 0.0)
        b = a * b_s + b
        a = a * a_s
        k *= 2
    h = b + a * h_sc[...]
    h_sc[...] = h[ts - 1:ts, :]
    o_ref[...] = (h * _gelu_tanh(ug_ref[...])).astype(o_ref.dtype)


def _rglru(ux, ug, cw, cb, wa, ba, wx, bx, lam, batch, ts):
    t, c = ux.shape
    nt = (t // batch) // ts
    row = lambda b, s: (b * nt + s, 0)
    const = lambda b, s: (0, 0)
    return pl.pallas_call(
        _rglru_kernel,
        grid=(batch, nt),
        in_specs=[pl.BlockSpec((ts, c), row), pl.BlockSpec((ts, c), row),
                  pl.BlockSpec((CONV_WIDTH, c), const), pl.BlockSpec((1, c), const),
                  pl.BlockSpec((c, c), const), pl.BlockSpec((1, c), const),
                  pl.BlockSpec((c, c), const), pl.BlockSpec((1, c), const),
                  pl.BlockSpec((1, c), const)],
        out_specs=pl.BlockSpec((ts, c), row),
        out_shape=jax.ShapeDtypeStruct((t, c), BF16),
        scratch_shapes=[pltpu.VMEM((ts + 8, c), F32), pltpu.VMEM((1, c), F32)],
        compiler_params=pltpu.CompilerParams(dimension_semantics=("arbitrary", "arbitrary")),
        name="rglru",
    )(ux, ug, cw, cb, wa, ba, wx, bx, lam)


def _block_diag(w):
    g, n, _ = w.shape
    eye = jnp.eye(g, dtype=w.dtype)
    return (eye[:, None, :, None] * w[:, :, None, :]).reshape(g * n, g * n)


def _bucket_table(tile):
    i = np.arange(tile)[:, None]
    j = np.arange(tile)[None, :]

    def bucket(n):
        nf = np.maximum(n, 1).astype(np.float32)
        large = REL_MAX_EXACT + (np.log(nf / REL_MAX_EXACT) / math.log(REL_MAX_DIST / REL_MAX_EXACT)
                                 * (REL_BUCKETS - REL_MAX_EXACT)).astype(np.int32)
        return np.where(n < REL_MAX_EXACT, n, np.minimum(large, REL_BUCKETS - 1))

    diag = np.where(j <= i, bucket(np.maximum(i - j, 0)), -1)
    left = bucket(tile + i - j)
    return np.stack([diag, left]).astype(np.int32)


def _bias_kernel(rb_ref, bucket_ref, o_ref):
    h = pl.program_id(0)
    bk = bucket_ref[...]
    far = rb_ref[REL_BUCKETS - 1, h]
    acc = jnp.zeros(bk.shape, F32)
    for b in range(REL_BUCKETS - 1):
        acc = jnp.where(bk == b, rb_ref[b, h] - far, acc)
    o_ref[0] = jnp.where(bk < 0, NEG, acc)


def _bias_tiles(rel_bias, tile):
    assert tile >= REL_MAX_DIST
    heads = rel_bias.shape[1]
    buckets = jnp.asarray(_bucket_table(tile))
    return pl.pallas_call(
        _bias_kernel,
        grid=(heads,),
        in_specs=[pl.BlockSpec(memory_space=pltpu.SMEM),
                  pl.BlockSpec((2, tile, tile), lambda h: (0, 0, 0))],
        out_specs=pl.BlockSpec((1, 2, tile, tile), lambda h: (h, 0, 0, 0)),
        out_shape=jax.ShapeDtypeStruct((heads, 2, tile, tile), F32),
        name="rel_bias_tiles",
    )(rel_bias, buckets)


def _attn_kernel(qi_tab, ki_tab, q_ref, k_ref, v_ref, bias_ref, lq1_ref, lk1_ref, lq2_ref, lk2_ref,
                 g_ref, o_ref, m_sc, l_sc, acc_sc, *, lambda_init):
    step_id = pl.program_id(2)
    qi = qi_tab[step_id]
    ki = ki_tab[step_id]

    @pl.when(ki == 0)
    def _():
        m_sc[...] = jnp.full(m_sc.shape, NEG, F32)
        l_sc[...] = jnp.zeros_like(l_sc)
        acc_sc[...] = jnp.zeros_like(acc_sc)

    def step(bias_slot):
        q = q_ref[0]
        k = k_ref[0]
        v = v_ref[0]
        for j in range(2):
            lo, hi = j * DIFF_QK_DIM, (j + 1) * DIFF_QK_DIM
            s = _dot_nt(q[:, lo:hi], k[:, lo:hi])
            if bias_slot is not None:
                s = s + bias_ref[0, bias_slot]
            m_prev = m_sc[j]
            m_new = jnp.maximum(m_prev, jnp.max(s, axis=-1, keepdims=True))
            alpha = jnp.exp(m_prev - m_new)
            p = jnp.exp(s - m_new)
            l_sc[j] = alpha * l_sc[j] + jnp.sum(p, axis=-1, keepdims=True)
            acc_sc[j] = alpha * acc_sc[j] + _dot(p.astype(BF16), v)
            m_sc[j] = m_new

    pl.when(ki == qi)(lambda: step(0))
    pl.when(ki == qi - 1)(lambda: step(1))
    pl.when(ki < qi - 1)(lambda: step(None))

    @pl.when(ki == qi)
    def _():
        lam = (jnp.exp(jnp.sum(lq1_ref[...] * lk1_ref[...], axis=-1, keepdims=True))
               - jnp.exp(jnp.sum(lq2_ref[...] * lk2_ref[...], axis=-1, keepdims=True))
               + lambda_init)
        o = acc_sc[0] / l_sc[0] - lam * (acc_sc[1] / l_sc[1])
        o_ref[0] = (_rms(o, g_ref[...]) * (1.0 - lambda_init)).astype(o_ref.dtype)


def _diff_attention(q, k, v, bias, lq1, lk1, lq2, lk2, g, lambda_init, tile):
    batch, s, width = q.shape
    heads = width // DIFF_V_DIM
    nq = s // tile
    pairs = [(a, b) for a in range(nq) for b in range(a + 1)]
    qi_tab = jnp.asarray([p[0] for p in pairs], jnp.int32)
    ki_tab = jnp.asarray([p[1] for p in pairs], jnp.int32)
    vec = lambda b, h, p, qt, kt: (0, 0)
    grid_spec = pltpu.PrefetchScalarGridSpec(
        num_scalar_prefetch=2,
        grid=(batch, heads, len(pairs)),
        in_specs=[pl.BlockSpec((1, tile, DIFF_V_DIM), lambda b, h, p, qt, kt: (b, qt[p], h)),
                  pl.BlockSpec((1, tile, DIFF_V_DIM), lambda b, h, p, qt, kt: (b, kt[p], h)),
                  pl.BlockSpec((1, tile, DIFF_V_DIM), lambda b, h, p, qt, kt: (b, kt[p], h)),
                  pl.BlockSpec((1, 2, tile, tile), lambda b, h, p, qt, kt: (h, 0, 0, 0)),
                  pl.BlockSpec((1, DIFF_QK_DIM), vec), pl.BlockSpec((1, DIFF_QK_DIM), vec),
                  pl.BlockSpec((1, DIFF_QK_DIM), vec), pl.BlockSpec((1, DIFF_QK_DIM), vec),
                  pl.BlockSpec((1, DIFF_V_DIM), vec)],
        out_specs=pl.BlockSpec((1, tile, DIFF_V_DIM), lambda b, h, p, qt, kt: (b, qt[p], h)),
        scratch_shapes=[pltpu.VMEM((2, tile, 1), F32), pltpu.VMEM((2, tile, 1), F32),
                        pltpu.VMEM((2, tile, DIFF_V_DIM), F32)],
    )
    return pl.pallas_call(
        functools.partial(_attn_kernel, lambda_init=lambda_init),
        grid_spec=grid_spec,
        out_shape=jax.ShapeDtypeStruct((batch, s, width), BF16),
        compiler_params=pltpu.CompilerParams(
            dimension_semantics=("arbitrary", "arbitrary", "arbitrary")),
        name="diff_attn",
    )(qi_tab, ki_tab, q, k, v, bias, lq1, lk1, lq2, lk2, g)


def _router_gates(logits, n_experts):
    lane = lax.broadcasted_iota(jnp.int32, logits.shape, 1)
    lane_f = lane.astype(F32)
    is_g = (lane >= n_experts) & (lane < n_experts + N_GROUPS)
    gl = jnp.where(is_g, logits, NEG)
    gmax = jnp.max(gl, axis=-1, keepdims=True)
    gidx = jnp.min(jnp.where(gl == gmax, lane_f - n_experts, 1e9), axis=-1, keepdims=True)
    gsum = jnp.sum(jnp.where(is_g, jnp.exp(gl - gmax), 0.0), axis=-1, keepdims=True)
    g_w = 1.0 / gsum
    in_grp = (lane < n_experts) & ((lane // EXPERTS_PER_GROUP).astype(F32) == gidx)
    el = jnp.where(in_grp, logits, NEG)
    t1 = jnp.max(el, axis=-1, keepdims=True)
    i1 = jnp.min(jnp.where(el == t1, lane_f, 1e9), axis=-1, keepdims=True)
    el2 = jnp.where(lane_f == i1, NEG, el)
    t2 = jnp.max(el2, axis=-1, keepdims=True)
    i2 = jnp.min(jnp.where(el2 == t2, lane_f, 1e9), axis=-1, keepdims=True)
    e2 = jnp.exp(t2 - t1)
    w1 = g_w / (1.0 + e2)
    w2 = g_w * e2 / (1.0 + e2)
    return jnp.where(lane_f == i1, w1, jnp.where(lane_f == i2, w2, 0.0))


def _out_kernel(tok_ref, mo_ref, h_ref, wo_ref, g_ref, wr_ref, h2_ref, hn_ref, gates_ref, *, n_experts):
    tw = tok_ref.shape[1]
    y = _dot(tok_ref[...], wo_ref[0:tw, :]) + _dot(mo_ref[...], wo_ref[tw:, :])
    h2 = h_ref[...] + y
    h2_ref[...] = h2
    hn = _rms(h2, g_ref[...]).astype(BF16)
    hn_ref[...] = hn
    gates_ref[...] = _router_gates(_dot(hn, wr_ref[...]), n_experts)


def _out_proj(tok, mo, h, wo_bf16, g, wr_bf16, n_experts, tm):
    t, d = h.shape
    tw, mw = tok.shape[1], mo.shape[1]
    row = lambda i: (i, 0)
    const = lambda i: (0, 0)
    return pl.pallas_call(
        functools.partial(_out_kernel, n_experts=n_experts),
        grid=(t // tm,),
        in_specs=[pl.BlockSpec((tm, tw), row), pl.BlockSpec((tm, mw), row), pl.BlockSpec((tm, d), row),
                  pl.BlockSpec((d, d), const), pl.BlockSpec((1, d), const),
                  pl.BlockSpec((d, LANES), const)],
        out_specs=[pl.BlockSpec((tm, d), row), pl.BlockSpec((tm, d), row), pl.BlockSpec((tm, LANES), row)],
        out_shape=[jax.ShapeDtypeStruct((t, d), F32), jax.ShapeDtypeStruct((t, d), BF16),
                   jax.ShapeDtypeStruct((t, LANES), F32)],
        name="out_proj_router",
    )(tok, mo, h, wo_bf16, g, wr_bf16)


def _moe_kernel(x_ref, gates_ref, h_ref, w13_ref, w2_ref, gfin_ref, o_ref, acc_sc, *, final_norm):
    e = pl.program_id(1)
    f = w2_ref.shape[1]

    @pl.when(e == 0)
    def _():
        acc_sc[...] = jnp.zeros_like(acc_sc)

    au = _dot(x_ref[...], w13_ref[0])
    a = au[:, :f]
    u = au[:, f:]
    gates = gates_ref[...]
    lane = lax.broadcasted_iota(jnp.int32, gates.shape, 1)
    gate = jnp.sum(jnp.where(lane == e, gates, 0.0), axis=-1, keepdims=True)
    hid = (a * jax.nn.sigmoid(a)) * u * gate
    acc_sc[...] += _dot(hid.astype(BF16), w2_ref[0])

    @pl.when(e == pl.num_programs(1) - 1)
    def _():
        out = h_ref[...] + acc_sc[...]
        if final_norm:
            out = _rms(out, gfin_ref[...])
        o_ref[...] = out


def _moe(x, gates, h, w13, w2, gfin, final_norm, tm):
    t, d = h.shape
    n_experts, f = w2.shape[0], w2.shape[1]
    row = lambda i, e: (i, 0)
    return pl.pallas_call(
        functools.partial(_moe_kernel, final_norm=final_norm),
        grid=(t // tm, n_experts),
        in_specs=[pl.BlockSpec((tm, d), row), pl.BlockSpec((tm, LANES), row), pl.BlockSpec((tm, d), row),
                  pl.BlockSpec((1, d, 2 * f), lambda i, e: (e, 0, 0)),
                  pl.BlockSpec((1, f, d), lambda i, e: (e, 0, 0)),
                  pl.BlockSpec((1, d), lambda i, e: (0, 0))],
        out_specs=pl.BlockSpec((tm, d), row),
        out_shape=jax.ShapeDtypeStruct((t, d), F32),
        scratch_shapes=[pltpu.VMEM((tm, d), F32)],
        compiler_params=pltpu.CompilerParams(dimension_semantics=("arbitrary", "arbitrary")),
        name="moe_ffn",
    )(x, gates, h, w13, w2, gfin)


def kernel(x, mem, norm_mix_g, w_in_a, w_in_b, w_out, conv_w, conv_b, gate_a_w, gate_a_b, gate_x_w, gate_x_b, lru_lambda, diff_lq1, diff_lk1, diff_lq2, diff_lk2, diff_subln_g, rel_bias, mem_norm_g, w_mem_kv, norm_ffn_g, w_group, w_router, w_exp1, w_exp3, w_exp2, final_norm_g):
    batch, seq, d = x.shape
    depth = w_out.shape[0]
    n_experts = w_router.shape[-1]
    t = batch * seq
    rnn_width = conv_w.shape[-1]
    tok_width = d - MEM_WIDTH
    tm = min(512, seq)
    tm_moe = min(1024, seq)
    ts = min(256, seq)
    attn_tile = min(512, seq)

    kv = _memkv(mem.reshape(-1, d), mem_norm_g.reshape(1, d), w_mem_kv.astype(BF16))
    kv = kv.reshape(depth, batch, mem.shape[1], 2 * MEM_WIDTH)
    bias = _bias_tiles(rel_bias, attn_tile)

    h = x.reshape(t, d)
    for i in range(depth):
        j = i // 2
        g_mix = norm_mix_g[i].reshape(1, d)
        if i % 2 == 0:
            ux, ug, mo = _in_proj(h, g_mix, w_in_a[j].astype(BF16), kv, i, batch,
                                  (rnn_width, rnn_width), (1.0, 1.0), (F32, F32), tm)
            tok = _rglru(ux, ug, conv_w[j], conv_b[j].reshape(1, -1),
                         _block_diag(gate_a_w[j]).astype(BF16), gate_a_b[j].reshape(1, -1),
                         _block_diag(gate_x_w[j]).astype(BF16), gate_x_b[j].reshape(1, -1),
                         lru_lambda[j].reshape(1, -1), batch, ts)
        else:
            q, k, v, mo = _in_proj(h, g_mix, w_in_b[j].astype(BF16), kv, i, batch,
                                   (tok_width, tok_width, tok_width),
                                   (DIFF_QK_DIM ** -0.5, 1.0, 1.0), (BF16, BF16, BF16), tm)
            lambda_init = 0.8 - 0.6 * math.exp(-0.3 * i)
            shp = (batch, seq, tok_width)
            tok = _diff_attention(q.reshape(shp), k.reshape(shp), v.reshape(shp), bias,
                                  diff_lq1[j].reshape(1, -1), diff_lk1[j].reshape(1, -1),
                                  diff_lq2[j].reshape(1, -1), diff_lk2[j].reshape(1, -1),
                                  diff_subln_g[j].reshape(1, -1), lambda_init, attn_tile)
            tok = tok.reshape(t, tok_width)
        wr = jnp.concatenate([w_router[i], w_group[i]], axis=-1)
        wr = jnp.pad(wr, ((0, 0), (0, LANES - wr.shape[1]))).astype(BF16)
        h2, hn, gates = _out_proj(tok, mo, h, w_out[i].astype(BF16), norm_ffn_g[i].reshape(1, d),
                                  wr, n_experts, tm)
        w13 = jnp.concatenate([w_exp1[i], w_exp3[i]], axis=-1).astype(BF16)
        h = _moe(hn, gates, h2, w13, w_exp2[i].astype(BF16), final_norm_g.reshape(1, d),
                 i == depth - 1, tm_moe)
    return h.reshape(batch, seq, d)
```

```python
import functools
import math

import numpy as np
import jax
import jax.numpy as jnp
from jax import lax
from jax.experimental import pallas as pl
from jax.experimental.pallas import tpu as pltpu

F32 = jnp.float32
BF16 = jnp.bfloat16
EPS = 1e-6
NEG = -1e30
LOG2E = math.log2(math.e)

MEM_HEADS = 4
MEM_HEAD_DIM = 64
MEM_WIDTH = MEM_HEADS * MEM_HEAD_DIM
CONV_WIDTH = 4
LRU_C = 8.0
DIFF_QK_DIM = 64
DIFF_V_DIM = 2 * DIFF_QK_DIM
REL_BUCKETS = 32
REL_MAX_EXACT = 16
REL_MAX_DIST = 128
N_GROUPS = 4
EXPERTS_PER_GROUP = 4
LANES = 128


def _rms(x, g):
    return x * lax.rsqrt(jnp.mean(x * x, axis=-1, keepdims=True) + EPS) * g


def _dot(a, b):
    return jnp.dot(a, b, preferred_element_type=F32)


def _dot_nt(a, b):
    return lax.dot_general(a, b, (((1,), (1,)), ((), ())), preferred_element_type=F32)


def _memkv_kernel(mem_ref, g_ref, w_ref, o_ref):
    xn = _rms(mem_ref[...], g_ref[...]).astype(BF16)
    o_ref[0] = _dot(xn, w_ref[0]).astype(BF16)


def _memkv(mem2d, g, w_bf16):
    depth, d, n = w_bf16.shape
    rows = mem2d.shape[0]
    return pl.pallas_call(
        _memkv_kernel,
        grid=(depth,),
        in_specs=[pl.BlockSpec((rows, d), lambda i: (0, 0)),
                  pl.BlockSpec((1, d), lambda i: (0, 0)),
                  pl.BlockSpec((1, d, n), lambda i: (i, 0, 0))],
        out_specs=pl.BlockSpec((1, rows, n), lambda i: (i, 0, 0)),
        out_shape=jax.ShapeDtypeStruct((depth, rows, n), BF16),
        name="memkv",
    )(mem2d, g, w_bf16)


def _mem_attn(qm, kv):
    outs = []
    for hh in range(MEM_HEADS):
        lo, hi = hh * MEM_HEAD_DIM, (hh + 1) * MEM_HEAD_DIM
        q = (qm[:, lo:hi] * (MEM_HEAD_DIM ** -0.5)).astype(BF16)
        k = kv[:, lo:hi]
        v = kv[:, MEM_WIDTH + lo:MEM_WIDTH + hi]
        s = _dot_nt(q, k)
        m = jnp.max(s, axis=-1, keepdims=True)
        p = jnp.exp(s - m)
        l = jnp.sum(p, axis=-1, keepdims=True)
        outs.append(_dot(p.astype(BF16), v) / l)
    return jnp.concatenate(outs, axis=-1)


def _in_kernel(h_ref, g_ref, w_ref, kv_ref, *out_refs, splits, scales):
    hn = _rms(h_ref[...], g_ref[...]).astype(BF16)
    proj = _dot(hn, w_ref[...])
    off = 0
    for ref, width, scale in zip(out_refs[:-1], splits, scales):
        piece = proj[:, off:off + width]
        if scale != 1.0:
            piece = piece * scale
        ref[...] = piece.astype(ref.dtype)
        off += width
    out_refs[-1][...] = _mem_attn(proj[:, off:off + MEM_WIDTH], kv_ref[0, 0]).astype(BF16)


def _in_proj(h, g, w_bf16, kv, layer, batch, splits, scales, dtypes, tm):
    t, d = h.shape
    n = w_bf16.shape[1]
    tiles_per_batch = (t // batch) // tm
    mem_len = kv.shape[2]
    out_shape = [jax.ShapeDtypeStruct((t, w), dt) for w, dt in zip(splits, dtypes)]
    out_shape.append(jax.ShapeDtypeStruct((t, MEM_WIDTH), BF16))
    out_specs = [pl.BlockSpec((tm, w), lambda i: (i, 0)) for w in splits]
    out_specs.append(pl.BlockSpec((tm, MEM_WIDTH), lambda i: (i, 0)))
    return pl.pallas_call(
        functools.partial(_in_kernel, splits=splits, scales=scales),
        grid=(t // tm,),
        in_specs=[pl.BlockSpec((tm, d), lambda i: (i, 0)),
                  pl.BlockSpec((1, d), lambda i: (0, 0)),
                  pl.BlockSpec((d, n), lambda i: (0, 0)),
                  pl.BlockSpec((1, 1, mem_len, 2 * MEM_WIDTH),
                               lambda i: (layer, i // tiles_per_batch, 0, 0))],
        out_specs=out_specs,
        out_shape=out_shape,
        name="in_proj",
    )(h, g, w_bf16, kv)


def _gelu_tanh(x):
    return 0.5 * x * (1.0 + jnp.tanh(math.sqrt(2.0 / math.pi) * (x + 0.044715 * (x * x * x))))


def _softplus(z):
    return jnp.maximum(z, 0.0) + jnp.log1p(jnp.exp(-jnp.abs(z)))


def _rglru_kernel(ux_ref, ug_ref, cw_ref, cb_ref, wa_ref, ba_ref, wx_ref, bx_ref, lam_ref,
                  o_ref, xpad_sc, h_sc):
    ts, c = ux_ref.shape
    pad = 8

    @pl.when(pl.program_id(1) == 0)
    def _():
        xpad_sc[0:pad, :] = jnp.zeros((pad, c), F32)
        h_sc[...] = jnp.zeros_like(h_sc)

    ux = ux_ref[...]
    xpad_sc[pad:pad + ts, :] = ux
    xc = cb_ref[...] + cw_ref[CONV_WIDTH - 1:CONV_WIDTH, :] * ux
    for j in range(CONV_WIDTH - 1):
        xc = xc + cw_ref[j:j + 1, :] * xpad_sc[pl.ds(pad - (CONV_WIDTH - 1) + j, ts), :]
    xpad_sc[0:pad, :] = ux[ts - pad:ts, :]

    xcb = xc.astype(BF16)
    r = jax.nn.sigmoid(_dot(xcb, wa_ref[...]) + ba_ref[...])
    ig = jax.nn.sigmoid(_dot(xcb, wx_ref[...]) + bx_ref[...])
    log_a = (-LRU_C) * r * _softplus(-lam_ref[...])
    a = jnp.exp(log_a)
    b = jnp.sqrt(-jnp.tanh(log_a) * (a * a + 1.0)) * (ig * xc)

    row = lax.broadcasted_iota(jnp.int32, (ts, c), 0)
    k = 1
    while k < ts:
        keep = row >= k
        a_s = jnp.where(keep, pltpu.roll(a, k, 0), 1.0)
        b_s = jnp.where(keep, pltpu.roll(b, k, 0), 0.0)
        b = a * b_s + b
        a = a * a_s
        k *= 2
    h = b + a * h_sc[...]
    h_sc[...] = h[ts - 1:ts, :]
    o_ref[...] = (h * _gelu_tanh(ug_ref[...])).astype(o_ref.dtype)


def _rglru(ux, ug, cw, cb, wa, ba, wx, bx, lam, batch, ts):
    t, c = ux.shape
    nt = (t // batch) // ts
    row = lambda b, s: (b * nt + s, 0)
    const = lambda b, s: (0, 0)
    return pl.pallas_call(
        _rglru_kernel,
        grid=(batch, nt),
        in_specs=[pl.BlockSpec((ts, c), row), pl.BlockSpec((ts, c), row),
                  pl.BlockSpec((CONV_WIDTH, c), const), pl.BlockSpec((1, c), const),
                  pl.BlockSpec((c, c), const), pl.BlockSpec((1, c), const),
                  pl.BlockSpec((c, c), const), pl.BlockSpec((1, c), const),
                  pl.BlockSpec((1, c), const)],
        out_specs=pl.BlockSpec((ts, c), row),
        out_shape=jax.ShapeDtypeStruct((t, c), BF16),
        scratch_shapes=[pltpu.VMEM((ts + 8, c), F32), pltpu.VMEM((1, c), F32)],
        compiler_params=pltpu.CompilerParams(dimension_semantics=("arbitrary", "arbitrary")),
        name="rglru",
    )(ux, ug, cw, cb, wa, ba, wx, bx, lam)


def _block_diag(w):
    g, n, _ = w.shape
    eye = jnp.eye(g, dtype=w.dtype)
    return (eye[:, None, :, None] * w[:, :, None, :]).reshape(g * n, g * n)


def _bucket_table(tile):
    i = np.arange(tile)[:, None]
    j = np.arange(tile)[None, :]

    def bucket(n):
        nf = np.maximum(n, 1).astype(np.float32)
        large = REL_MAX_EXACT + (np.log(nf / REL_MAX_EXACT) / math.log(REL_MAX_DIST / REL_MAX_EXACT)
                                 * (REL_BUCKETS - REL_MAX_EXACT)).astype(np.int32)
        return np.where(n < REL_MAX_EXACT, n, np.minimum(large, REL_BUCKETS - 1))

    diag = np.where(j <= i, bucket(np.maximum(i - j, 0)), -1)
    left = bucket(tile + i - j)
    return np.stack([diag, left]).astype(np.int32)


def _bias_kernel(rb_ref, bucket_ref, o_ref):
    h = pl.program_id(0)
    bk = bucket_ref[...]
    far = rb_ref[REL_BUCKETS - 1, h]
    acc = jnp.zeros(bk.shape, F32)
    for b in range(REL_BUCKETS - 1):
        acc = jnp.where(bk == b, rb_ref[b, h] - far, acc)
    o_ref[0] = jnp.where(bk < 0, NEG, acc * LOG2E)


def _bias_tiles(rel_bias, tile):
    assert tile >= REL_MAX_DIST
    heads = rel_bias.shape[1]
    buckets = jnp.asarray(_bucket_table(tile))
    return pl.pallas_call(
        _bias_kernel,
        grid=(heads,),
        in_specs=[pl.BlockSpec(memory_space=pltpu.SMEM),
                  pl.BlockSpec((2, tile, tile), lambda h: (0, 0, 0))],
        out_specs=pl.BlockSpec((1, 2, tile, tile), lambda h: (h, 0, 0, 0)),
        out_shape=jax.ShapeDtypeStruct((heads, 2, tile, tile), F32),
        name="rel_bias_tiles",
    )(rel_bias, buckets)


def _attn_kernel(q_ref, k_ref, v_ref, bias_ref, lq1_ref, lk1_ref, lq2_ref, lk2_ref,
                 g_ref, o_ref, m_sc, acc_sc, sa_sc, sb_sc, *, lambda_init):
    qi = pl.program_id(2)
    tile = q_ref.shape[1]
    reps = tile // LANES
    m_sc[...] = jnp.full(m_sc.shape, NEG, F32)
    acc_sc[...] = jnp.zeros_like(acc_sc)

    def scores(ki, s_ref):
        start = pl.multiple_of(ki * tile, tile)
        q = q_ref[0]
        k = k_ref[0, pl.ds(start, tile), :]
        for j in range(2):
            lo, hi = j * DIFF_QK_DIM, (j + 1) * DIFF_QK_DIM
            s_ref[j] = _dot_nt(q[:, lo:hi], k[:, lo:hi])

    def accumulate(ki, s_ref, bias_slot):
        start = pl.multiple_of(ki * tile, tile)
        v = v_ref[0, pl.ds(start, tile), :]
        v1 = jnp.concatenate([v, jnp.ones_like(v)], axis=-1)
        for j in range(2):
            s = s_ref[j]
            if bias_slot is not None:
                s = s + bias_ref[0, bias_slot]
            m_prev = m_sc[j]
            m_new = jnp.maximum(m_prev, jnp.max(s, axis=-1, keepdims=True))
            alpha = jnp.exp2(m_prev - m_new)
            p = jnp.exp2(s - pltpu.repeat(m_new, reps, axis=1))
            acc_sc[j] = pltpu.repeat(alpha, 2, axis=1) * acc_sc[j] + _dot(p.astype(BF16), v1)
            m_sc[j] = m_new

    n_far = jnp.maximum(qi - 1, 0)
    odd = n_far & 1
    pl.when(odd == 0)(lambda: scores(0, sa_sc))

    @pl.when(odd == 1)
    def _():
        scores(0, sb_sc)
        scores(1, sa_sc)
        accumulate(0, sb_sc, None)

    def far_pair(p, carry):
        base = odd + 2 * p
        scores(base + 1, sb_sc)
        accumulate(base, sa_sc, None)
        scores(base + 2, sa_sc)
        accumulate(base + 1, sb_sc, None)
        return carry

    lax.fori_loop(0, n_far >> 1, far_pair, 0)

    @pl.when(qi >= 1)
    def _():
        scores(qi, sb_sc)
        accumulate(qi - 1, sa_sc, 1)
        accumulate(qi, sb_sc, 0)

    @pl.when(qi == 0)
    def _():
        accumulate(0, sa_sc, 0)

    lam = (jnp.exp(jnp.sum(lq1_ref[...] * lk1_ref[...], axis=-1, keepdims=True))
           - jnp.exp(jnp.sum(lq2_ref[...] * lk2_ref[...], axis=-1, keepdims=True))
           + lambda_init)
    dv = o_ref.shape[-1]
    o = (acc_sc[0, :, :dv] / acc_sc[0, :, dv:]) - lam * (acc_sc[1, :, :dv] / acc_sc[1, :, dv:])
    o_ref[0] = (_rms(o, g_ref[...]) * (1.0 - lambda_init)).astype(o_ref.dtype)


def _diff_attention(q, k, v, bias, lq1, lk1, lq2, lk2, g, lambda_init, tile):
    batch, s, width = q.shape
    heads = width // DIFF_V_DIM
    vec = lambda b, h, i: (0, 0)
    head_all = lambda b, h, i: (b, 0, h)
    return pl.pallas_call(
        functools.partial(_attn_kernel, lambda_init=lambda_init),
        grid=(batch, heads, s // tile),
        in_specs=[pl.BlockSpec((1, tile, DIFF_V_DIM), lambda b, h, i: (b, i, h)),
                  pl.BlockSpec((1, s, DIFF_V_DIM), head_all),
                  pl.BlockSpec((1, s, DIFF_V_DIM), head_all),
                  pl.BlockSpec((1, 2, tile, tile), lambda b, h, i: (h, 0, 0, 0)),
                  pl.BlockSpec((1, DIFF_QK_DIM), vec), pl.BlockSpec((1, DIFF_QK_DIM), vec),
                  pl.BlockSpec((1, DIFF_QK_DIM), vec), pl.BlockSpec((1, DIFF_QK_DIM), vec),
                  pl.BlockSpec((1, DIFF_V_DIM), vec)],
        out_specs=pl.BlockSpec((1, tile, DIFF_V_DIM), lambda b, h, i: (b, i, h)),
        out_shape=jax.ShapeDtypeStruct((batch, s, width), BF16),
        scratch_shapes=[pltpu.VMEM((2, tile, LANES), F32),
                        pltpu.VMEM((2, tile, 2 * DIFF_V_DIM), F32),
                        pltpu.VMEM((2, tile, tile), F32), pltpu.VMEM((2, tile, tile), F32)],
        compiler_params=pltpu.CompilerParams(
            dimension_semantics=("arbitrary", "arbitrary", "arbitrary")),
        name="diff_attn",
    )(q, k, v, bias, lq1, lk1, lq2, lk2, g)


def _router_gates(logits, n_experts):
    lane = lax.broadcasted_iota(jnp.int32, logits.shape, 1)
    lane_f = lane.astype(F32)
    is_g = (lane >= n_experts) & (lane < n_experts + N_GROUPS)
    gl = jnp.where(is_g, logits, NEG)
    gmax = jnp.max(gl, axis=-1, keepdims=True)
    gidx = jnp.min(jnp.where(gl == gmax, lane_f - n_experts, 1e9), axis=-1, keepdims=True)
    gsum = jnp.sum(jnp.where(is_g, jnp.exp(gl - gmax), 0.0), axis=-1, keepdims=True)
    g_w = 1.0 / gsum
    in_grp = (lane < n_experts) & ((lane // EXPERTS_PER_GROUP).astype(F32) == gidx)
    el = jnp.where(in_grp, logits, NEG)
    t1 = jnp.max(el, axis=-1, keepdims=True)
    i1 = jnp.min(jnp.where(el == t1, lane_f, 1e9), axis=-1, keepdims=True)
    el2 = jnp.where(lane_f == i1, NEG, el)
    t2 = jnp.max(el2, axis=-1, keepdims=True)
    i2 = jnp.min(jnp.where(el2 == t2, lane_f, 1e9), axis=-1, keepdims=True)
    e2 = jnp.exp(t2 - t1)
    w1 = g_w / (1.0 + e2)
    w2 = g_w * e2 / (1.0 + e2)
    return jnp.where(lane_f == i1, w1, jnp.where(lane_f == i2, w2, 0.0))


def _out_kernel(tok_ref, mo_ref, h_ref, wo_ref, g_ref, wr_ref, h2_ref, hn_ref, gates_ref, *, n_experts):
    tw = tok_ref.shape[1]
    y = _dot(tok_ref[...], wo_ref[0:tw, :]) + _dot(mo_ref[...], wo_ref[tw:, :])
    h2 = h_ref[...] + y
    h2_ref[...] = h2
    hn = _rms(h2, g_ref[...]).astype(BF16)
    hn_ref[...] = hn
    gates_ref[...] = _router_gates(_dot(hn, wr_ref[...]), n_experts)


def _out_proj(tok, mo, h, wo_bf16, g, wr_bf16, n_experts, tm):
    t, d = h.shape
    tw, mw = tok.shape[1], mo.shape[1]
    row = lambda i: (i, 0)
    const = lambda i: (0, 0)
    return pl.pallas_call(
        functools.partial(_out_kernel, n_experts=n_experts),
        grid=(t // tm,),
        in_specs=[pl.BlockSpec((tm, tw), row), pl.BlockSpec((tm, mw), row), pl.BlockSpec((tm, d), row),
                  pl.BlockSpec((d, d), const), pl.BlockSpec((1, d), const),
                  pl.BlockSpec((d, LANES), const)],
        out_specs=[pl.BlockSpec((tm, d), row), pl.BlockSpec((tm, d), row), pl.BlockSpec((tm, LANES), row)],
        out_shape=[jax.ShapeDtypeStruct((t, d), F32), jax.ShapeDtypeStruct((t, d), BF16),
                   jax.ShapeDtypeStruct((t, LANES), F32)],
        name="out_proj_router",
    )(tok, mo, h, wo_bf16, g, wr_bf16)


def _moe_kernel(x_ref, gates_ref, h_ref, w13_ref, w2_ref, gfin_ref, o_ref, acc_sc, *, final_norm):
    e = pl.program_id(1)
    f = w2_ref.shape[1]

    @pl.when(e == 0)
    def _():
        acc_sc[...] = jnp.zeros_like(acc_sc)

    au = _dot(x_ref[...], w13_ref[0])
    a = au[:, :f]
    u = au[:, f:]
    gates = gates_ref[...]
    lane = lax.broadcasted_iota(jnp.int32, gates.shape, 1)
    gate = jnp.sum(jnp.where(lane == e, gates, 0.0), axis=-1, keepdims=True)
    hid = (a * jax.nn.sigmoid(a)) * u * gate
    acc_sc[...] += _dot(hid.astype(BF16), w2_ref[0])

    @pl.when(e == pl.num_programs(1) - 1)
    def _():
        out = h_ref[...] + acc_sc[...]
        if final_norm:
            out = _rms(out, gfin_ref[...])
        o_ref[...] = out


def _moe(x, gates, h, w13, w2, gfin, final_norm, tm):
    t, d = h.shape
    n_experts, f = w2.shape[0], w2.shape[1]
    row = lambda i, e: (i, 0)
    return pl.pallas_call(
        functools.partial(_moe_kernel, final_norm=final_norm),
        grid=(t // tm, n_experts),
        in_specs=[pl.BlockSpec((tm, d), row), pl.BlockSpec((tm, LANES), row), pl.BlockSpec((tm, d), row),
                  pl.BlockSpec((1, d, 2 * f), lambda i, e: (e, 0, 0)),
                  pl.BlockSpec((1, f, d), lambda i, e: (e, 0, 0)),
                  pl.BlockSpec((1, d), lambda i, e: (0, 0))],
        out_specs=pl.BlockSpec((tm, d), row),
        out_shape=jax.ShapeDtypeStruct((t, d), F32),
        scratch_shapes=[pltpu.VMEM((tm, d), F32)],
        compiler_params=pltpu.CompilerParams(dimension_semantics=("arbitrary", "arbitrary")),
        name="moe_ffn",
    )(x, gates, h, w13, w2, gfin)


def kernel(x, mem, norm_mix_g, w_in_a, w_in_b, w_out, conv_w, conv_b, gate_a_w, gate_a_b, gate_x_w, gate_x_b, lru_lambda, diff_lq1, diff_lk1, diff_lq2, diff_lk2, diff_subln_g, rel_bias, mem_norm_g, w_mem_kv, norm_ffn_g, w_group, w_router, w_exp1, w_exp3, w_exp2, final_norm_g):
    batch, seq, d = x.shape
    depth = w_out.shape[0]
    n_experts = w_router.shape[-1]
    t = batch * seq
    rnn_width = conv_w.shape[-1]
    tok_width = d - MEM_WIDTH
    tm = min(512, seq)
    tm_moe = min(1024, seq)
    ts = min(256, seq)
    attn_tile = min(512, seq)

    kv = _memkv(mem.reshape(-1, d), mem_norm_g.reshape(1, d), w_mem_kv.astype(BF16))
    kv = kv.reshape(depth, batch, mem.shape[1], 2 * MEM_WIDTH)
    bias = _bias_tiles(rel_bias, attn_tile)

    h = x.reshape(t, d)
    for i in range(depth):
        j = i // 2
        g_mix = norm_mix_g[i].reshape(1, d)
        if i % 2 == 0:
            ux, ug, mo = _in_proj(h, g_mix, w_in_a[j].astype(BF16), kv, i, batch,
                                  (rnn_width, rnn_width), (1.0, 1.0), (F32, F32), tm)
            tok = _rglru(ux, ug, conv_w[j], conv_b[j].reshape(1, -1),
                         _block_diag(gate_a_w[j]).astype(BF16), gate_a_b[j].reshape(1, -1),
                         _block_diag(gate_x_w[j]).astype(BF16), gate_x_b[j].reshape(1, -1),
                         lru_lambda[j].reshape(1, -1), batch, ts)
        else:
            q, k, v, mo = _in_proj(h, g_mix, w_in_b[j].astype(BF16), kv, i, batch,
                                   (tok_width, tok_width, tok_width),
                                   (DIFF_QK_DIM ** -0.5 * LOG2E, 1.0, 1.0), (BF16, BF16, BF16), tm)
            lambda_init = 0.8 - 0.6 * math.exp(-0.3 * i)
            shp = (batch, seq, tok_width)
            tok = _diff_attention(q.reshape(shp), k.reshape(shp), v.reshape(shp), bias,
                                  diff_lq1[j].reshape(1, -1), diff_lk1[j].reshape(1, -1),
                                  diff_lq2[j].reshape(1, -1), diff_lk2[j].reshape(1, -1),
                                  diff_subln_g[j].reshape(1, -1), lambda_init, attn_tile)
            tok = tok.reshape(t, tok_width)
        wr = jnp.concatenate([w_router[i], w_group[i]], axis=-1)
        wr = jnp.pad(wr, ((0, 0), (0, LANES - wr.shape[1]))).astype(BF16)
        h2, hn, gates = _out_proj(tok, mo, h, w_out[i].astype(BF16), norm_ffn_g[i].reshape(1, d),
                                  wr, n_experts, tm)
        w13 = jnp.concatenate([w_exp1[i], w_exp3[i]], axis=-1).astype(BF16)
        h = _moe(hn, gates, h2, w13, w_exp2[i].astype(BF16), final_norm_g.reshape(1, d),
                 i == depth - 1, tm_moe)
    return h.reshape(batch, seq, d)
```

```python
import functools
import math

import numpy as np
import jax
import jax.numpy as jnp
from jax import lax
from jax.experimental import pallas as pl
from jax.experimental.pallas import tpu as pltpu

F32 = jnp.float32
BF16 = jnp.bfloat16
EPS = 1e-6
NEG = -1e30
LOG2E = math.log2(math.e)

MEM_HEADS = 4
MEM_HEAD_DIM = 64
MEM_WIDTH = MEM_HEADS * MEM_HEAD_DIM
CONV_WIDTH = 4
LRU_C = 8.0
DIFF_QK_DIM = 64
DIFF_V_DIM = 2 * DIFF_QK_DIM
REL_BUCKETS = 32
REL_MAX_EXACT = 16
REL_MAX_DIST = 128
N_GROUPS = 4
EXPERTS_PER_GROUP = 4
LANES = 128


def _rms(x, g):
    return x * lax.rsqrt(jnp.mean(x * x, axis=-1, keepdims=True) + EPS) * g


def _dot(a, b):
    return jnp.dot(a, b, preferred_element_type=F32)


def _dot_nt(a, b):
    return lax.dot_general(a, b, (((1,), (1,)), ((), ())), preferred_element_type=F32)


def _memkv_kernel(mem_ref, g_ref, w_ref, o_ref):
    xn = _rms(mem_ref[...], g_ref[...]).astype(BF16)
    o_ref[0] = _dot(xn, w_ref[0]).astype(BF16)


def _memkv(mem2d, g, w_bf16):
    depth, d, n = w_bf16.shape
    rows = mem2d.shape[0]
    return pl.pallas_call(
        _memkv_kernel,
        grid=(depth,),
        in_specs=[pl.BlockSpec((rows, d), lambda i: (0, 0)),
                  pl.BlockSpec((1, d), lambda i: (0, 0)),
                  pl.BlockSpec((1, d, n), lambda i: (i, 0, 0))],
        out_specs=pl.BlockSpec((1, rows, n), lambda i: (i, 0, 0)),
        out_shape=jax.ShapeDtypeStruct((depth, rows, n), BF16),
        name="memkv",
    )(mem2d, g, w_bf16)


def _row_copy_wait(hbm_ref, vmem_ref, sem):
    n = vmem_ref.shape[0]
    pltpu.make_async_copy(hbm_ref.at[pl.ds(0, n)], vmem_ref, sem).wait()


def _gather_rows(pos_ref, src_hbm, dst_vmem, sem):
    def body(r, carry):
        pltpu.make_async_copy(src_hbm.at[pl.ds(pos_ref[0, 0, r], 1)], dst_vmem.at[pl.ds(r, 1)], sem).start()
        return carry

    lax.fori_loop(0, dst_vmem.shape[0], body, 0, unroll=8)
    _row_copy_wait(src_hbm, dst_vmem, sem)


def _mem_attn(qm, kv):
    outs = []
    for hh in range(MEM_HEADS):
        lo, hi = hh * MEM_HEAD_DIM, (hh + 1) * MEM_HEAD_DIM
        q = (qm[:, lo:hi] * (MEM_HEAD_DIM ** -0.5)).astype(BF16)
        k = kv[:, lo:hi]
        v = kv[:, MEM_WIDTH + lo:MEM_WIDTH + hi]
        s = _dot_nt(q, k)
        m = jnp.max(s, axis=-1, keepdims=True)
        p = jnp.exp(s - m)
        l = jnp.sum(p, axis=-1, keepdims=True)
        outs.append(_dot(p.astype(BF16), v) / l)
    return jnp.concatenate(outs, axis=-1)


def _project(h, g_ref, w_ref, kv_ref, out_refs, splits, scales):
    hn = _rms(h, g_ref[...]).astype(BF16)
    proj = _dot(hn, w_ref[...])
    off = 0
    for ref, width, scale in zip(out_refs[:-1], splits, scales):
        piece = proj[:, off:off + width]
        if scale != 1.0:
            piece = piece * scale
        ref[...] = piece.astype(ref.dtype)
        off += width
    out_refs[-1][...] = _mem_attn(proj[:, off:off + MEM_WIDTH], kv_ref[0, 0]).astype(BF16)


def _in_kernel(h_ref, g_ref, w_ref, kv_ref, *out_refs, splits, scales):
    _project(h_ref[...], g_ref, w_ref, kv_ref, out_refs, splits, scales)


def _in_moe_kernel(pos_ref, h2_ref, ys_ref, g_ref, w_ref, kv_ref, h_ref, *rest, splits, scales):
    out_refs, (ybuf, sem) = rest[:-2], rest[-2:]
    _gather_rows(pos_ref, ys_ref, ybuf, sem)
    h = h2_ref[...] + ybuf[...]
    h_ref[...] = h
    _project(h, g_ref, w_ref, kv_ref, out_refs, splits, scales)


def _in_proj(h, moe, g, w_bf16, kv, layer, batch, splits, scales, dtypes, tm):
    t, d = h.shape
    n = w_bf16.shape[1]
    tiles_per_batch = (t // batch) // tm
    mem_len = kv.shape[2]
    row = lambda i: (i, 0)
    const = lambda i: (0, 0)
    out_shape = [jax.ShapeDtypeStruct((t, w), dt) for w, dt in zip(splits, dtypes)]
    out_shape.append(jax.ShapeDtypeStruct((t, MEM_WIDTH), BF16))
    out_specs = [pl.BlockSpec((tm, w), row) for w in splits]
    out_specs.append(pl.BlockSpec((tm, MEM_WIDTH), row))
    in_specs = [pl.BlockSpec((tm, d), row),
                pl.BlockSpec((1, d), const),
                pl.BlockSpec((d, n), const),
                pl.BlockSpec((1, 1, mem_len, 2 * MEM_WIDTH),
                             lambda i: (layer, i // tiles_per_batch, 0, 0))]
    if moe is None:
        return pl.pallas_call(
            functools.partial(_in_kernel, splits=splits, scales=scales),
            grid=(t // tm,), in_specs=in_specs, out_specs=out_specs, out_shape=out_shape,
            name="in_proj",
        )(h, g, w_bf16, kv)
    ys, pos = moe
    in_specs = ([pl.BlockSpec((1, 1, tm), lambda i: (i, 0, 0), memory_space=pltpu.SMEM), in_specs[0],
                 pl.BlockSpec(memory_space=pl.ANY)] + in_specs[1:])
    return pl.pallas_call(
        functools.partial(_in_moe_kernel, splits=splits, scales=scales),
        grid=(t // tm,), in_specs=in_specs,
        out_specs=[pl.BlockSpec((tm, d), row)] + out_specs,
        out_shape=[jax.ShapeDtypeStruct((t, d), F32)] + out_shape,
        scratch_shapes=[pltpu.VMEM((tm, d), F32), pltpu.SemaphoreType.DMA],
        name="in_proj_moe_gather",
    )(pos.reshape(t // tm, 1, tm), h, ys, g, w_bf16, kv)


def _gelu_tanh(x):
    return 0.5 * x * (1.0 + jnp.tanh(math.sqrt(2.0 / math.pi) * (x + 0.044715 * (x * x * x))))


def _softplus(z):
    return jnp.maximum(z, 0.0) + jnp.log1p(jnp.exp(-jnp.abs(z)))


def _rglru_kernel(ux_ref, ug_ref, cw_ref, cb_ref, wa_ref, ba_ref, wx_ref, bx_ref, lam_ref,
                  o_ref, xpad_sc, h_sc):
    ts, c = ux_ref.shape
    pad = 8

    @pl.when(pl.program_id(1) == 0)
    def _():
        xpad_sc[0:pad, :] = jnp.zeros((pad, c), F32)
        h_sc[...] = jnp.zeros_like(h_sc)

    ux = ux_ref[...]
    xpad_sc[pad:pad + ts, :] = ux
    xc = cb_ref[...] + cw_ref[CONV_WIDTH - 1:CONV_WIDTH, :] * ux
    for j in range(CONV_WIDTH - 1):
        xc = xc + cw_ref[j:j + 1, :] * xpad_sc[pl.ds(pad - (CONV_WIDTH - 1) + j, ts), :]
    xpad_sc[0:pad, :] = ux[ts - pad:ts, :]

    xcb = xc.astype(BF16)
    r = jax.nn.sigmoid(_dot(xcb, wa_ref[...]) + ba_ref[...])
    ig = jax.nn.sigmoid(_dot(xcb, wx_ref[...]) + bx_ref[...])
    log_a = (-LRU_C) * r * _softplus(-lam_ref[...])
    a = jnp.exp(log_a)
    b = jnp.sqrt(-jnp.tanh(log_a) * (a * a + 1.0)) * (ig * xc)

    row = lax.broadcasted_iota(jnp.int32, (ts, c), 0)
    k = 1
    while k < ts:
        keep = row >= k
        a_s = jnp.where(keep, pltpu.roll(a, k, 0), 1.0)
        b_s = jnp.where(keep, pltpu.roll(b, k, 0), 0.0)
        b = a * b_s + b
        a = a * a_s
        k *= 2
    h = b + a * h_sc[...]
    h_sc[...] = h[ts - 1:ts, :]
    o_ref[...] = (h * _gelu_tanh(ug_ref[...])).astype(o_ref.dtype)


def _rglru(ux, ug, cw, cb, wa, ba, wx, bx, lam, batch, ts):
    t, c = ux.shape
    nt = (t // batch) // ts
    row = lambda b, s: (b * nt + s, 0)
    const = lambda b, s: (0, 0)
    return pl.pallas_call(
        _rglru_kernel,
        grid=(batch, nt),
        in_specs=[pl.BlockSpec((ts, c), row), pl.BlockSpec((ts, c), row),
                  pl.BlockSpec((CONV_WIDTH, c), const), pl.BlockSpec((1, c), const),
                  pl.BlockSpec((c, c), const), pl.BlockSpec((1, c), const),
                  pl.BlockSpec((c, c), const), pl.BlockSpec((1, c), const),
                  pl.BlockSpec((1, c), const)],
        out_specs=pl.BlockSpec((ts, c), row),
        out_shape=jax.ShapeDtypeStruct((t, c), BF16),
        scratch_shapes=[pltpu.VMEM((ts + 8, c), F32), pltpu.VMEM((1, c), F32)],
        compiler_params=pltpu.CompilerParams(dimension_semantics=("arbitrary", "arbitrary")),
        name="rglru",
    )(ux, ug, cw, cb, wa, ba, wx, bx, lam)


def _block_diag(w):
    g, n, _ = w.shape
    eye = jnp.eye(g, dtype=w.dtype)
    return (eye[:, None, :, None] * w[:, :, None, :]).reshape(g * n, g * n)


def _bucket_table(tile):
    i = np.arange(tile)[:, None]
    j = np.arange(tile)[None, :]

    def bucket(n):
        nf = np.maximum(n, 1).astype(np.float32)
        large = REL_MAX_EXACT + (np.log(nf / REL_MAX_EXACT) / math.log(REL_MAX_DIST / REL_MAX_EXACT)
                                 * (REL_BUCKETS - REL_MAX_EXACT)).astype(np.int32)
        return np.where(n < REL_MAX_EXACT, n, np.minimum(large, REL_BUCKETS - 1))

    diag = np.where(j <= i, bucket(np.maximum(i - j, 0)), -1)
    left = bucket(tile + i - j)
    return np.stack([diag, left]).astype(np.int32)


def _bias_kernel(rb_ref, bucket_ref, o_ref):
    h = pl.program_id(0)
    bk = bucket_ref[...]
    far = rb_ref[REL_BUCKETS - 1, h]
    acc = jnp.zeros(bk.shape, F32)
    for b in range(REL_BUCKETS - 1):
        acc = jnp.where(bk == b, rb_ref[b, h] - far, acc)
    o_ref[0] = jnp.where(bk < 0, NEG, acc * LOG2E)


def _bias_tiles(rel_bias, tile):
    assert tile >= REL_MAX_DIST
    heads = rel_bias.shape[1]
    buckets = jnp.asarray(_bucket_table(tile))
    return pl.pallas_call(
        _bias_kernel,
        grid=(heads,),
        in_specs=[pl.BlockSpec(memory_space=pltpu.SMEM),
                  pl.BlockSpec((2, tile, tile), lambda h: (0, 0, 0))],
        out_specs=pl.BlockSpec((1, 2, tile, tile), lambda h: (h, 0, 0, 0)),
        out_shape=jax.ShapeDtypeStruct((heads, 2, tile, tile), F32),
        name="rel_bias_tiles",
    )(rel_bias, buckets)


def _attn_kernel(q_ref, k_ref, v_ref, bias_ref, lq1_ref, lk1_ref, lq2_ref, lk2_ref,
                 g_ref, o_ref, m_sc, acc_sc, sa_sc, sb_sc, *, lambda_init):
    qi = pl.program_id(2)
    tile = q_ref.shape[1]
    reps = tile // LANES
    m_sc[...] = jnp.full(m_sc.shape, NEG, F32)
    acc_sc[...] = jnp.zeros_like(acc_sc)

    def scores(ki, s_ref):
        start = pl.multiple_of(ki * tile, tile)
        q = q_ref[0]
        k = k_ref[0, pl.ds(start, tile), :]
        for j in range(2):
            lo, hi = j * DIFF_QK_DIM, (j + 1) * DIFF_QK_DIM
            s_ref[j] = _dot_nt(q[:, lo:hi], k[:, lo:hi])

    def accumulate(ki, s_ref, bias_slot):
        start = pl.multiple_of(ki * tile, tile)
        v = v_ref[0, pl.ds(start, tile), :]
        v1 = jnp.concatenate([v, jnp.ones_like(v)], axis=-1)
        for j in range(2):
            s = s_ref[j]
            if bias_slot is not None:
                s = s + bias_ref[0, bias_slot]
            m_prev = m_sc[j]
            m_new = jnp.maximum(m_prev, jnp.max(s, axis=-1, keepdims=True))
            alpha = jnp.exp2(m_prev - m_new)
            p = jnp.exp2(s - jnp.concatenate([m_new] * reps, axis=1))
            acc_sc[j] = jnp.concatenate([alpha, alpha], axis=1) * acc_sc[j] + _dot(p.astype(BF16), v1)
            m_sc[j] = m_new

    n_far = jnp.maximum(qi - 1, 0)
    odd = n_far & 1
    pl.when(odd == 0)(lambda: scores(0, sa_sc))

    @pl.when(odd == 1)
    def _():
        scores(0, sb_sc)
        scores(1, sa_sc)
        accumulate(0, sb_sc, None)

    def far_pair(p, carry):
        base = odd + 2 * p
        scores(base + 1, sb_sc)
        accumulate(base, sa_sc, None)
        scores(base + 2, sa_sc)
        accumulate(base + 1, sb_sc, None)
        return carry

    lax.fori_loop(0, n_far >> 1, far_pair, 0)

    @pl.when(qi >= 1)
    def _():
        scores(qi, sb_sc)
        accumulate(qi - 1, sa_sc, 1)
        accumulate(qi, sb_sc, 0)

    @pl.when(qi == 0)
    def _():
        accumulate(0, sa_sc, 0)

    lam = (jnp.exp(jnp.sum(lq1_ref[...] * lk1_ref[...], axis=-1, keepdims=True))
           - jnp.exp(jnp.sum(lq2_ref[...] * lk2_ref[...], axis=-1, keepdims=True))
           + lambda_init)
    dv = o_ref.shape[-1]
    o = (acc_sc[0, :, :dv] / acc_sc[0, :, dv:]) - lam * (acc_sc[1, :, :dv] / acc_sc[1, :, dv:])
    o_ref[0] = (_rms(o, g_ref[...]) * (1.0 - lambda_init)).astype(o_ref.dtype)


def _diff_attention(q, k, v, bias, lq1, lk1, lq2, lk2, g, lambda_init, tile):
    batch, s, width = q.shape
    heads = width // DIFF_V_DIM
    vec = lambda b, h, i: (0, 0)
    head_all = lambda b, h, i: (b, 0, h)
    return pl.pallas_call(
        functools.partial(_attn_kernel, lambda_init=lambda_init),
        grid=(batch, heads, s // tile),
        in_specs=[pl.BlockSpec((1, tile, DIFF_V_DIM), lambda b, h, i: (b, i, h)),
                  pl.BlockSpec((1, s, DIFF_V_DIM), head_all),
                  pl.BlockSpec((1, s, DIFF_V_DIM), head_all),
                  pl.BlockSpec((1, 2, tile, tile), lambda b, h, i: (h, 0, 0, 0)),
                  pl.BlockSpec((1, DIFF_QK_DIM), vec), pl.BlockSpec((1, DIFF_QK_DIM), vec),
                  pl.BlockSpec((1, DIFF_QK_DIM), vec), pl.BlockSpec((1, DIFF_QK_DIM), vec),
                  pl.BlockSpec((1, DIFF_V_DIM), vec)],
        out_specs=pl.BlockSpec((1, tile, DIFF_V_DIM), lambda b, h, i: (b, i, h)),
        out_shape=jax.ShapeDtypeStruct((batch, s, width), BF16),
        scratch_shapes=[pltpu.VMEM((2, tile, LANES), F32),
                        pltpu.VMEM((2, tile, 2 * DIFF_V_DIM), F32),
                        pltpu.VMEM((2, tile, tile), F32), pltpu.VMEM((2, tile, tile), F32)],
        compiler_params=pltpu.CompilerParams(
            dimension_semantics=("arbitrary", "arbitrary", "arbitrary")),
        name="diff_attn",
    )(q, k, v, bias, lq1, lk1, lq2, lk2, g)


def _router_gates(logits, n_experts):
    lane = lax.broadcasted_iota(jnp.int32, logits.shape, 1)
    lane_f = lane.astype(F32)
    is_g = (lane >= n_experts) & (lane < n_experts + N_GROUPS)
    gl = jnp.where(is_g, logits, NEG)
    gmax = jnp.max(gl, axis=-1, keepdims=True)
    gidx = jnp.min(jnp.where(gl == gmax, lane_f - n_experts, 1e9), axis=-1, keepdims=True)
    gsum = jnp.sum(jnp.where(is_g, jnp.exp(gl - gmax), 0.0), axis=-1, keepdims=True)
    g_w = 1.0 / gsum
    in_grp = (lane < n_experts) & ((lane // EXPERTS_PER_GROUP).astype(F32) == gidx)
    el = jnp.where(in_grp, logits, NEG)
    t1 = jnp.max(el, axis=-1, keepdims=True)
    i1 = jnp.min(jnp.where(el == t1, lane_f, 1e9), axis=-1, keepdims=True)
    el2 = jnp.where(lane_f == i1, NEG, el)
    t2 = jnp.max(el2, axis=-1, keepdims=True)
    i2 = jnp.min(jnp.where(el2 == t2, lane_f, 1e9), axis=-1, keepdims=True)
    e2 = jnp.exp(t2 - t1)
    w1 = g_w / (1.0 + e2)
    w2 = g_w * e2 / (1.0 + e2)
    return jnp.where(lane_f == i1, w1, jnp.where(lane_f == i2, w2, 0.0)), gidx


def _out_kernel(tok_ref, mo_ref, h_ref, wo_ref, g_ref, wr_ref, h2_ref, hn_ref, route_ref, cnt_sc, *, n_experts):
    @pl.when(pl.program_id(0) == 0)
    def _():
        cnt_sc[...] = jnp.zeros_like(cnt_sc)

    tw = tok_ref.shape[1]
    y = _dot(tok_ref[...], wo_ref[0:tw, :]) + _dot(mo_ref[...], wo_ref[tw:, :])
    h2 = h_ref[...] + y
    h2_ref[...] = h2
    hn = _rms(h2, g_ref[...])
    hn_ref[...] = hn
    _, gidx = _router_gates(_dot(hn.astype(BF16), wr_ref[...]), n_experts)

    tm = h2.shape[0]
    lane = lax.broadcasted_iota(jnp.int32, (tm, LANES), 1)
    onehot = jnp.where(lane.astype(F32) == gidx, 1.0, 0.0)
    earlier = lax.broadcasted_iota(jnp.int32, (tm, tm), 1) < lax.broadcasted_iota(jnp.int32, (tm, tm), 0)
    before = _dot(jnp.where(earlier, 1.0, 0.0).astype(BF16), onehot.astype(BF16)) + cnt_sc[...]
    rank = jnp.sum(onehot * before, axis=-1, keepdims=True)
    cnt_sc[...] += jnp.sum(onehot, axis=0, keepdims=True)
    route_ref[...] = jnp.where(lane == 0, gidx, jnp.where(lane == 1, rank, 0.0))


def _out_proj(tok, mo, h, wo_bf16, g, wr_bf16, n_experts, tm):
    t, d = h.shape
    tw, mw = tok.shape[1], mo.shape[1]
    row = lambda i: (i, 0)
    const = lambda i: (0, 0)
    return pl.pallas_call(
        functools.partial(_out_kernel, n_experts=n_experts),
        grid=(t // tm,),
        in_specs=[pl.BlockSpec((tm, tw), row), pl.BlockSpec((tm, mw), row), pl.BlockSpec((tm, d), row),
                  pl.BlockSpec((d, d), const), pl.BlockSpec((1, d), const),
                  pl.BlockSpec((d, LANES), const)],
        out_specs=[pl.BlockSpec((tm, d), row), pl.BlockSpec((tm, d), row), pl.BlockSpec((tm, LANES), row)],
        out_shape=[jax.ShapeDtypeStruct((t, d), F32), jax.ShapeDtypeStruct((t, d), F32),
                   jax.ShapeDtypeStruct((t, LANES), F32)],
        scratch_shapes=[pltpu.VMEM((1, LANES), F32)],
        compiler_params=pltpu.CompilerParams(dimension_semantics=("arbitrary",)),
        name="out_proj_router",
    )(tok, mo, h, wo_bf16, g, wr_bf16)


def _route_tables(route, tile, n_tiles):
    gidx = route[:, 0].astype(jnp.int32)
    rank = route[:, 1].astype(jnp.int32)
    onehot = gidx[:, None] == jnp.arange(N_GROUPS, dtype=jnp.int32)[None, :]
    cnt = jnp.sum(onehot, axis=0, dtype=jnp.int32)
    ptiles = (cnt + tile - 1) // tile
    tile_end = jnp.cumsum(ptiles)
    off = (tile_end - ptiles) * tile
    pos = jnp.sum(jnp.where(onehot, off[None, :], 0), axis=1) + rank
    tiles = jnp.arange(n_tiles, dtype=jnp.int32)
    tile_group = jnp.minimum(jnp.sum(tiles[:, None] >= tile_end[None, :], axis=1), N_GROUPS - 1)
    return pos.astype(jnp.int32), tile_group.astype(jnp.int32), tile_end[-1:].astype(jnp.int32)


def _sort_kernel(pos_ref, x_ref, xs_in_ref, xs_ref, sem):
    del xs_in_ref

    def body(r, carry):
        pltpu.make_async_copy(x_ref.at[pl.ds(r, 1)], xs_ref.at[pl.ds(pos_ref[0, 0, r], 1)], sem).start()
        return carry

    n = x_ref.shape[0]
    lax.fori_loop(0, n, body, 0, unroll=8)
    pltpu.make_async_copy(x_ref, xs_ref.at[pl.ds(0, n)], sem).wait()


def _sort_rows(x, pos, n_rows, tm):
    t, d = x.shape
    return pl.pallas_call(
        _sort_kernel,
        grid=(t // tm,),
        in_specs=[pl.BlockSpec((1, 1, tm), lambda i: (i, 0, 0), memory_space=pltpu.SMEM),
                  pl.BlockSpec((tm, d), lambda i: (i, 0)),
                  pl.BlockSpec(memory_space=pl.ANY)],
        out_specs=pl.BlockSpec(memory_space=pl.ANY),
        out_shape=jax.ShapeDtypeStruct((n_rows, d), x.dtype),
        input_output_aliases={2: 0},
        scratch_shapes=[pltpu.SemaphoreType.DMA],
        compiler_params=pltpu.CompilerParams(dimension_semantics=("arbitrary",)),
        name="moe_sort_rows",
    )(pos.reshape(t // tm, 1, tm), x, jnp.zeros((n_rows, d), x.dtype))


def _experts_kernel(tg_ref, na_ref, xs_ref, wr_ref, w13_ref, w2_ref, ys_ref, *, n_experts):
    i = pl.program_id(0)

    @pl.when(i >= na_ref[0])
    def _():
        ys_ref[...] = jnp.zeros_like(ys_ref)

    @pl.when(i < na_ref[0])
    def _():
        x = xs_ref[...].astype(BF16)
        gates, _ = _router_gates(_dot(x, wr_ref[...]), n_experts)
        lane = lax.broadcasted_iota(jnp.int32, gates.shape, 1)
        first = tg_ref[i] * EXPERTS_PER_GROUP
        f = w2_ref.shape[1]
        hid = []
        for j in range(EXPERTS_PER_GROUP):
            au = _dot(x, w13_ref[j])
            a = au[:, :f]
            u = au[:, f:]
            gate = jnp.sum(jnp.where(lane == first + j, gates, 0.0), axis=-1, keepdims=True)
            hid.append(((a * jax.nn.sigmoid(a)) * u * gate).astype(BF16))
        w2 = w2_ref[...]
        ys_ref[...] = _dot(jnp.concatenate(hid, axis=-1), w2.reshape(-1, w2.shape[-1]))


def _experts(xs, tile_group, n_active, wr, w13, w2, n_experts, tile):
    rows, d = xs.shape
    f = w2.shape[1]
    epg = EXPERTS_PER_GROUP
    grid_spec = pltpu.PrefetchScalarGridSpec(
        num_scalar_prefetch=2,
        grid=(rows // tile,),
        in_specs=[pl.BlockSpec((tile, d), lambda i, tg, na: (i, 0)),
                  pl.BlockSpec((d, LANES), lambda i, tg, na: (0, 0)),
                  pl.BlockSpec((epg, d, 2 * f), lambda i, tg, na: (tg[i], 0, 0)),
                  pl.BlockSpec((epg, f, d), lambda i, tg, na: (tg[i], 0, 0))],
        out_specs=pl.BlockSpec((tile, d), lambda i, tg, na: (i, 0)),
    )
    return pl.pallas_call(
        functools.partial(_experts_kernel, n_experts=n_experts),
        grid_spec=grid_spec,
        out_shape=jax.ShapeDtypeStruct((rows, d), F32),
        compiler_params=pltpu.CompilerParams(dimension_semantics=("arbitrary",)),
        name="moe_experts",
    )(tile_group, n_active, xs, wr, w13, w2)


def _final_kernel(pos_ref, h2_ref, ys_ref, g_ref, o_ref, ybuf, sem):
    _gather_rows(pos_ref, ys_ref, ybuf, sem)
    o_ref[...] = _rms(h2_ref[...] + ybuf[...], g_ref[...])


def _final_norm(h2, ys, pos, g, tm):
    t, d = h2.shape
    return pl.pallas_call(
        _final_kernel,
        grid=(t // tm,),
        in_specs=[pl.BlockSpec((1, 1, tm), lambda i: (i, 0, 0), memory_space=pltpu.SMEM),
                  pl.BlockSpec((tm, d), lambda i: (i, 0)),
                  pl.BlockSpec(memory_space=pl.ANY),
                  pl.BlockSpec((1, d), lambda i: (0, 0))],
        out_specs=pl.BlockSpec((tm, d), lambda i: (i, 0)),
        out_shape=jax.ShapeDtypeStruct((t, d), F32),
        scratch_shapes=[pltpu.VMEM((tm, d), F32), pltpu.SemaphoreType.DMA],
        name="final_norm_moe_gather",
    )(pos.reshape(t // tm, 1, tm), h2, ys, g)


def kernel(x, mem, norm_mix_g, w_in_a, w_in_b, w_out, conv_w, conv_b, gate_a_w, gate_a_b, gate_x_w, gate_x_b, lru_lambda, diff_lq1, diff_lk1, diff_lq2, diff_lk2, diff_subln_g, rel_bias, mem_norm_g, w_mem_kv, norm_ffn_g, w_group, w_router, w_exp1, w_exp3, w_exp2, final_norm_g):
    batch, seq, d = x.shape
    depth = w_out.shape[0]
    n_experts = w_router.shape[-1]
    t = batch * seq
    rnn_width = conv_w.shape[-1]
    tok_width = d - MEM_WIDTH
    tm = min(512, seq)
    ts = min(256, seq)
    attn_tile = min(512, seq)
    moe_tile = min(512, seq)
    moe_tiles = t // moe_tile + N_GROUPS

    kv = _memkv(mem.reshape(-1, d), mem_norm_g.reshape(1, d), w_mem_kv.astype(BF16))
    kv = kv.reshape(depth, batch, mem.shape[1], 2 * MEM_WIDTH)
    bias = _bias_tiles(rel_bias, attn_tile)

    h = x.reshape(t, d)
    moe = None
    for i in range(depth):
        j = i // 2
        g_mix = norm_mix_g[i].reshape(1, d)
        if i % 2 == 0:
            outs = _in_proj(h, moe, g_mix, w_in_a[j].astype(BF16), kv, i, batch,
                            (rnn_width, rnn_width), (1.0, 1.0), (F32, F32), tm)
            if moe is not None:
                h, outs = outs[0], outs[1:]
            ux, ug, mo = outs
            tok = _rglru(ux, ug, conv_w[j], conv_b[j].reshape(1, -1),
                         _block_diag(gate_a_w[j]).astype(BF16), gate_a_b[j].reshape(1, -1),
                         _block_diag(gate_x_w[j]).astype(BF16), gate_x_b[j].reshape(1, -1),
                         lru_lambda[j].reshape(1, -1), batch, ts)
        else:
            outs = _in_proj(h, moe, g_mix, w_in_b[j].astype(BF16), kv, i, batch,
                            (tok_width, tok_width, tok_width),
                            (DIFF_QK_DIM ** -0.5 * LOG2E, 1.0, 1.0), (BF16, BF16, BF16), tm)
            if moe is not None:
                h, outs = outs[0], outs[1:]
            q, k, v, mo = outs
            lambda_init = 0.8 - 0.6 * math.exp(-0.3 * i)
            shp = (batch, seq, tok_width)
            tok = _diff_attention(q.reshape(shp), k.reshape(shp), v.reshape(shp), bias,
                                  diff_lq1[j].reshape(1, -1), diff_lk1[j].reshape(1, -1),
                                  diff_lq2[j].reshape(1, -1), diff_lk2[j].reshape(1, -1),
                                  diff_subln_g[j].reshape(1, -1), lambda_init, attn_tile)
            tok = tok.reshape(t, tok_width)
        wr = jnp.concatenate([w_router[i], w_group[i]], axis=-1)
        wr = jnp.pad(wr, ((0, 0), (0, LANES - wr.shape[1]))).astype(BF16)
        h, hn, route = _out_proj(tok, mo, h, w_out[i].astype(BF16), norm_ffn_g[i].reshape(1, d),
                                 wr, n_experts, tm)
        pos, tile_group, n_active = _route_tables(route, moe_tile, moe_tiles)
        xs = _sort_rows(hn, pos, moe_tiles * moe_tile, tm)
        w13 = jnp.concatenate([w_exp1[i], w_exp3[i]], axis=-1).astype(BF16)
        ys = _experts(xs, tile_group, n_active, wr, w13, w_exp2[i].astype(BF16), n_experts, moe_tile)
        moe = (ys, pos)
    return _final_norm(h, moe[0], moe[1], final_norm_g.reshape(1, d), tm).reshape(batch, seq, d)
```

```python
import functools
import math

import numpy as np
import jax
import jax.numpy as jnp
from jax import lax
from jax.experimental import pallas as pl
from jax.experimental.pallas import tpu as pltpu

F32 = jnp.float32
BF16 = jnp.bfloat16
EPS = 1e-6
NEG = -1e30
LOG2E = math.log2(math.e)

MEM_HEADS = 4
MEM_HEAD_DIM = 64
MEM_WIDTH = MEM_HEADS * MEM_HEAD_DIM
CONV_WIDTH = 4
LRU_C = 8.0
DIFF_QK_DIM = 64
DIFF_V_DIM = 2 * DIFF_QK_DIM
REL_BUCKETS = 32
REL_MAX_EXACT = 16
REL_MAX_DIST = 128
N_GROUPS = 4
EXPERTS_PER_GROUP = 4
LANES = 128


def _rms(x, g):
    return x * lax.rsqrt(jnp.mean(x * x, axis=-1, keepdims=True) + EPS) * g


def _dot(a, b):
    return jnp.dot(a, b, preferred_element_type=F32)


def _dot_nt(a, b):
    return lax.dot_general(a, b, (((1,), (1,)), ((), ())), preferred_element_type=F32)


def _memkv_kernel(mem_ref, g_ref, w_ref, o_ref):
    xn = _rms(mem_ref[...], g_ref[...]).astype(BF16)
    o_ref[0] = _dot(xn, w_ref[0].astype(BF16)).astype(BF16)


def _memkv(mem2d, g, w):
    depth, d, n = w.shape
    rows = mem2d.shape[0]
    return pl.pallas_call(
        _memkv_kernel,
        grid=(depth,),
        in_specs=[pl.BlockSpec((rows, d), lambda i: (0, 0)),
                  pl.BlockSpec((1, d), lambda i: (0, 0)),
                  pl.BlockSpec((1, d, n), lambda i: (i, 0, 0))],
        out_specs=pl.BlockSpec((1, rows, n), lambda i: (i, 0, 0)),
        out_shape=jax.ShapeDtypeStruct((depth, rows, n), BF16),
        name="memkv",
    )(mem2d, g, w)


def _row_copy_wait(hbm_ref, vmem_ref, sem):
    n = vmem_ref.shape[0]
    pltpu.make_async_copy(hbm_ref.at[pl.ds(0, n)], vmem_ref, sem).wait()


def _start_row_copies(n, make_copy):
    def body(p, carry):
        for u in range(2):
            make_copy(2 * p + u).start(priority=u)
        return carry

    lax.fori_loop(0, n // 2, body, 0, unroll=4)


def _gather_rows(pos_ref, src_hbm, dst_vmem, sem):
    def make_copy(r):
        return pltpu.make_async_copy(src_hbm.at[pl.ds(pos_ref[0, 0, r], 1)], dst_vmem.at[pl.ds(r, 1)], sem)

    _start_row_copies(dst_vmem.shape[0], make_copy)
    _row_copy_wait(src_hbm, dst_vmem, sem)


def _mem_attn(qm, kv):
    outs = []
    for hh in range(MEM_HEADS):
        lo, hi = hh * MEM_HEAD_DIM, (hh + 1) * MEM_HEAD_DIM
        q = (qm[:, lo:hi] * (MEM_HEAD_DIM ** -0.5)).astype(BF16)
        k = kv[:, lo:hi]
        v = kv[:, MEM_WIDTH + lo:MEM_WIDTH + hi]
        s = _dot_nt(q, k)
        m = jnp.max(s, axis=-1, keepdims=True)
        p = jnp.exp(s - m)
        l = jnp.sum(p, axis=-1, keepdims=True)
        outs.append(_dot(p.astype(BF16), v) / l)
    return jnp.concatenate(outs, axis=-1)


def _cast_weight_once(w_ref, wb_sc):
    @pl.when(pl.program_id(0) == 0)
    def _():
        wb_sc[...] = w_ref[0].astype(BF16)


def _project(h, g_ref, wb_sc, kv_ref, out_refs, splits, scales):
    hn = _rms(h, g_ref[...]).astype(BF16)
    proj = _dot(hn, wb_sc[...])
    off = 0
    for ref, width, scale in zip(out_refs[:-1], splits, scales):
        piece = proj[:, off:off + width]
        if scale != 1.0:
            piece = piece * scale
        ref[...] = piece.astype(ref.dtype)
        off += width
    out_refs[-1][...] = _mem_attn(proj[:, off:off + MEM_WIDTH], kv_ref[0, 0]).astype(BF16)


def _in_kernel(h_ref, g_ref, w_ref, kv_ref, *rest, splits, scales):
    out_refs, wb_sc = rest[:-1], rest[-1]
    _cast_weight_once(w_ref, wb_sc)
    _project(h_ref[...], g_ref, wb_sc, kv_ref, out_refs, splits, scales)


def _in_moe_kernel(pos_ref, h2_ref, ys_ref, g_ref, w_ref, kv_ref, h_ref, *rest, splits, scales):
    out_refs, (ybuf, sem, wb_sc) = rest[:-3], rest[-3:]
    _cast_weight_once(w_ref, wb_sc)
    _gather_rows(pos_ref, ys_ref, ybuf, sem)
    h = h2_ref[...] + ybuf[...]
    h_ref[...] = h
    _project(h, g_ref, wb_sc, kv_ref, out_refs, splits, scales)


def _in_proj(h, moe, g, w_all, w_idx, kv, layer, batch, splits, scales, dtypes, tm):
    t, d = h.shape
    n = w_all.shape[2]
    tiles_per_batch = (t // batch) // tm
    mem_len = kv.shape[2]
    row = lambda i: (i, 0)
    const = lambda i: (0, 0)
    out_shape = [jax.ShapeDtypeStruct((t, w), dt) for w, dt in zip(splits, dtypes)]
    out_shape.append(jax.ShapeDtypeStruct((t, MEM_WIDTH), BF16))
    out_specs = [pl.BlockSpec((tm, w), row) for w in splits]
    out_specs.append(pl.BlockSpec((tm, MEM_WIDTH), row))
    in_specs = [pl.BlockSpec((tm, d), row),
                pl.BlockSpec((1, d), const),
                pl.BlockSpec((1, d, n), lambda i: (w_idx, 0, 0), pipeline_mode=pl.Buffered(1)),
                pl.BlockSpec((1, 1, mem_len, 2 * MEM_WIDTH),
                             lambda i: (layer, i // tiles_per_batch, 0, 0))]
    if moe is None:
        return pl.pallas_call(
            functools.partial(_in_kernel, splits=splits, scales=scales),
            grid=(t // tm,), in_specs=in_specs, out_specs=out_specs, out_shape=out_shape,
            scratch_shapes=[pltpu.VMEM((d, n), BF16)],
            compiler_params=pltpu.CompilerParams(dimension_semantics=("arbitrary",)),
            name="in_proj",
        )(h, g, w_all, kv)
    ys, pos = moe
    in_specs = ([pl.BlockSpec((1, 1, tm), lambda i: (i, 0, 0), memory_space=pltpu.SMEM), in_specs[0],
                 pl.BlockSpec(memory_space=pl.ANY)] + in_specs[1:])
    return pl.pallas_call(
        functools.partial(_in_moe_kernel, splits=splits, scales=scales),
        grid=(t // tm,), in_specs=in_specs,
        out_specs=[pl.BlockSpec((tm, d), row)] + out_specs,
        out_shape=[jax.ShapeDtypeStruct((t, d), F32)] + out_shape,
        scratch_shapes=[pltpu.VMEM((tm, d), F32), pltpu.SemaphoreType.DMA, pltpu.VMEM((d, n), BF16)],
        compiler_params=pltpu.CompilerParams(dimension_semantics=("arbitrary",)),
        name="in_proj_moe_gather",
    )(pos.reshape(t // tm, 1, tm), h, ys, g, w_all, kv)


def _gelu_tanh(x):
    return 0.5 * x * (1.0 + jnp.tanh(math.sqrt(2.0 / math.pi) * (x + 0.044715 * (x * x * x))))


def _softplus(z):
    return jnp.maximum(z, 0.0) + jnp.log1p(jnp.exp(-jnp.abs(z)))


def _rglru_kernel(ux_ref, ug_ref, cw_ref, cb_ref, wa_ref, ba_ref, wx_ref, bx_ref, lam_ref,
                  o_ref, xpad_sc, h_sc):
    ts, c = ux_ref.shape
    pad = 8

    @pl.when(pl.program_id(1) == 0)
    def _():
        xpad_sc[0:pad, :] = jnp.zeros((pad, c), F32)
        h_sc[...] = jnp.zeros_like(h_sc)

    ux = ux_ref[...]
    xpad_sc[pad:pad + ts, :] = ux
    xc = cb_ref[...] + cw_ref[CONV_WIDTH - 1:CONV_WIDTH, :] * ux
    for j in range(CONV_WIDTH - 1):
        xc = xc + cw_ref[j:j + 1, :] * xpad_sc[pl.ds(pad - (CONV_WIDTH - 1) + j, ts), :]
    xpad_sc[0:pad, :] = ux[ts - pad:ts, :]

    xcb = xc.astype(BF16)
    r = jax.nn.sigmoid(_dot(xcb, wa_ref[...]) + ba_ref[...])
    ig = jax.nn.sigmoid(_dot(xcb, wx_ref[...]) + bx_ref[...])
    log_a = (-LRU_C) * r * _softplus(-lam_ref[...])
    a = jnp.exp(log_a)
    b = jnp.sqrt(-jnp.tanh(log_a) * (a * a + 1.0)) * (ig * xc)

    row = lax.broadcasted_iota(jnp.int32, (ts, c), 0)
    k = 1
    while k < ts:
        keep = row >= k
        a_s = jnp.where(keep, pltpu.roll(a, k, 0), 1.0)
        b_s = jnp.where(keep, pltpu.roll(b, k, 0), 0.0)
        b = a * b_s + b
        a = a * a_s
        k *= 2
    h = b + a * h_sc[...]
    h_sc[...] = h[ts - 1:ts, :]
    o_ref[...] = (h * _gelu_tanh(ug_ref[...])).astype(o_ref.dtype)


def _rglru(ux, ug, cw, cb, wa, ba, wx, bx, lam, batch, ts):
    t, c = ux.shape
    nt = (t // batch) // ts
    row = lambda b, s: (b * nt + s, 0)
    const = lambda b, s: (0, 0)
    return pl.pallas_call(
        _rglru_kernel,
        grid=(batch, nt),
        in_specs=[pl.BlockSpec((ts, c), row), pl.BlockSpec((ts, c), row),
                  pl.BlockSpec((CONV_WIDTH, c), const), pl.BlockSpec((1, c), const),
                  pl.BlockSpec((c, c), const), pl.BlockSpec((1, c), const),
                  pl.BlockSpec((c, c), const), pl.BlockSpec((1, c), const),
                  pl.BlockSpec((1, c), const)],
        out_specs=pl.BlockSpec((ts, c), row),
        out_shape=jax.ShapeDtypeStruct((t, c), BF16),
        scratch_shapes=[pltpu.VMEM((ts + 8, c), F32), pltpu.VMEM((1, c), F32)],
        compiler_params=pltpu.CompilerParams(dimension_semantics=("arbitrary", "arbitrary")),
        name="rglru",
    )(ux, ug, cw, cb, wa, ba, wx, bx, lam)


def _block_diag(w):
    g, n, _ = w.shape
    eye = jnp.eye(g, dtype=w.dtype)
    return (eye[:, None, :, None] * w[:, :, None, :]).reshape(g * n, g * n)


def _bucket_table(tile):
    i = np.arange(tile)[:, None]
    j = np.arange(tile)[None, :]

    def bucket(n):
        nf = np.maximum(n, 1).astype(np.float32)
        large = REL_MAX_EXACT + (np.log(nf / REL_MAX_EXACT) / math.log(REL_MAX_DIST / REL_MAX_EXACT)
                                 * (REL_BUCKETS - REL_MAX_EXACT)).astype(np.int32)
        return np.where(n < REL_MAX_EXACT, n, np.minimum(large, REL_BUCKETS - 1))

    diag = np.where(j <= i, bucket(np.maximum(i - j, 0)), -1)
    left = bucket(tile + i - j)
    return np.stack([diag, left]).astype(np.int32)


def _bias_kernel(rb_ref, bucket_ref, o_ref):
    h = pl.program_id(0)
    bk = bucket_ref[...]
    far = rb_ref[REL_BUCKETS - 1, h]
    acc = jnp.zeros(bk.shape, F32)
    for b in range(REL_BUCKETS - 1):
        acc = jnp.where(bk == b, rb_ref[b, h] - far, acc)
    o_ref[0] = jnp.where(bk < 0, NEG, acc * LOG2E)


def _bias_tiles(rel_bias, tile):
    assert tile >= REL_MAX_DIST
    heads = rel_bias.shape[1]
    buckets = jnp.asarray(_bucket_table(tile))
    return pl.pallas_call(
        _bias_kernel,
        grid=(heads,),
        in_specs=[pl.BlockSpec(memory_space=pltpu.SMEM),
                  pl.BlockSpec((2, tile, tile), lambda h: (0, 0, 0))],
        out_specs=pl.BlockSpec((1, 2, tile, tile), lambda h: (h, 0, 0, 0)),
        out_shape=jax.ShapeDtypeStruct((heads, 2, tile, tile), F32),
        name="rel_bias_tiles",
    )(rel_bias, buckets)


def _attn_kernel(q_ref, k_ref, v_ref, bias_ref, lq1_ref, lk1_ref, lq2_ref, lk2_ref,
                 g_ref, o_ref, m_sc, acc_sc, sa_sc, sb_sc, *, lambda_init):
    qi = pl.program_id(2)
    tile = q_ref.shape[1]
    reps = tile // LANES
    m_sc[...] = jnp.full(m_sc.shape, NEG, F32)
    acc_sc[...] = jnp.zeros_like(acc_sc)

    def scores(ki, s_ref):
        start = pl.multiple_of(ki * tile, tile)
        q = q_ref[0]
        k = k_ref[0, pl.ds(start, tile), :]
        for j in range(2):
            lo, hi = j * DIFF_QK_DIM, (j + 1) * DIFF_QK_DIM
            s_ref[j] = _dot_nt(q[:, lo:hi], k[:, lo:hi])

    def accumulate(ki, s_ref, bias_slot):
        start = pl.multiple_of(ki * tile, tile)
        v = v_ref[0, pl.ds(start, tile), :]
        v1 = jnp.concatenate([v, jnp.ones_like(v)], axis=-1)
        for j in range(2):
            s = s_ref[j]
            if bias_slot is not None:
                s = s + bias_ref[0, bias_slot]
            m_prev = m_sc[j]
            m_new = jnp.maximum(m_prev, jnp.max(s, axis=-1, keepdims=True))
            alpha = jnp.exp2(m_prev - m_new)
            p = jnp.exp2(s - jnp.concatenate([m_new] * reps, axis=1))
            acc_sc[j] = jnp.concatenate([alpha, alpha], axis=1) * acc_sc[j] + _dot(p.astype(BF16), v1)
            m_sc[j] = m_new

    n_far = jnp.maximum(qi - 1, 0)
    odd = n_far & 1
    pl.when(odd == 0)(lambda: scores(0, sa_sc))

    @pl.when(odd == 1)
    def _():
        scores(0, sb_sc)
        scores(1, sa_sc)
        accumulate(0, sb_sc, None)

    def far_pair(p, carry):
        base = odd + 2 * p
        scores(base + 1, sb_sc)
        accumulate(base, sa_sc, None)
        scores(base + 2, sa_sc)
        accumulate(base + 1, sb_sc, None)
        return carry

    lax.fori_loop(0, n_far >> 1, far_pair, 0)

    @pl.when(qi >= 1)
    def _():
        scores(qi, sb_sc)
        accumulate(qi - 1, sa_sc, 1)
        accumulate(qi, sb_sc, 0)

    @pl.when(qi == 0)
    def _():
        accumulate(0, sa_sc, 0)

    lam = (jnp.exp(jnp.sum(lq1_ref[...] * lk1_ref[...], axis=-1, keepdims=True))
           - jnp.exp(jnp.sum(lq2_ref[...] * lk2_ref[...], axis=-1, keepdims=True))
           + lambda_init)
    dv = o_ref.shape[-1]
    o = (acc_sc[0, :, :dv] / acc_sc[0, :, dv:]) - lam * (acc_sc[1, :, :dv] / acc_sc[1, :, dv:])
    o_ref[0] = (_rms(o, g_ref[...]) * (1.0 - lambda_init)).astype(o_ref.dtype)


def _diff_attention(q, k, v, bias, lq1, lk1, lq2, lk2, g, lambda_init, tile):
    batch, s, width = q.shape
    heads = width // DIFF_V_DIM
    vec = lambda b, h, i: (0, 0)
    head_all = lambda b, h, i: (b, 0, h)
    return pl.pallas_call(
        functools.partial(_attn_kernel, lambda_init=lambda_init),
        grid=(batch, heads, s // tile),
        in_specs=[pl.BlockSpec((1, tile, DIFF_V_DIM), lambda b, h, i: (b, i, h)),
                  pl.BlockSpec((1, s, DIFF_V_DIM), head_all),
                  pl.BlockSpec((1, s, DIFF_V_DIM), head_all),
                  pl.BlockSpec((1, 2, tile, tile), lambda b, h, i: (h, 0, 0, 0)),
                  pl.BlockSpec((1, DIFF_QK_DIM), vec), pl.BlockSpec((1, DIFF_QK_DIM), vec),
                  pl.BlockSpec((1, DIFF_QK_DIM), vec), pl.BlockSpec((1, DIFF_QK_DIM), vec),
                  pl.BlockSpec((1, DIFF_V_DIM), vec)],
        out_specs=pl.BlockSpec((1, tile, DIFF_V_DIM), lambda b, h, i: (b, i, h)),
        out_shape=jax.ShapeDtypeStruct((batch, s, width), BF16),
        scratch_shapes=[pltpu.VMEM((2, tile, LANES), F32),
                        pltpu.VMEM((2, tile, 2 * DIFF_V_DIM), F32),
                        pltpu.VMEM((2, tile, tile), F32), pltpu.VMEM((2, tile, tile), F32)],
        compiler_params=pltpu.CompilerParams(
            dimension_semantics=("arbitrary", "arbitrary", "arbitrary")),
        name="diff_attn",
    )(q, k, v, bias, lq1, lk1, lq2, lk2, g)


def _router_gates(logits, n_experts):
    lane = lax.broadcasted_iota(jnp.int32, logits.shape, 1)
    lane_f = lane.astype(F32)
    is_g = (lane >= n_experts) & (lane < n_experts + N_GROUPS)
    gl = jnp.where(is_g, logits, NEG)
    gmax = jnp.max(gl, axis=-1, keepdims=True)
    gidx = jnp.min(jnp.where(gl == gmax, lane_f - n_experts, 1e9), axis=-1, keepdims=True)
    gsum = jnp.sum(jnp.where(is_g, jnp.exp(gl - gmax), 0.0), axis=-1, keepdims=True)
    g_w = 1.0 / gsum
    in_grp = (lane < n_experts) & ((lane // EXPERTS_PER_GROUP).astype(F32) == gidx)
    el = jnp.where(in_grp, logits, NEG)
    t1 = jnp.max(el, axis=-1, keepdims=True)
    i1 = jnp.min(jnp.where(el == t1, lane_f, 1e9), axis=-1, keepdims=True)
    el2 = jnp.where(lane_f == i1, NEG, el)
    t2 = jnp.max(el2, axis=-1, keepdims=True)
    i2 = jnp.min(jnp.where(el2 == t2, lane_f, 1e9), axis=-1, keepdims=True)
    e2 = jnp.exp(t2 - t1)
    w1 = g_w / (1.0 + e2)
    w2 = g_w * e2 / (1.0 + e2)
    return jnp.where(lane_f == i1, w1, jnp.where(lane_f == i2, w2, 0.0)), gidx


def _out_kernel(tok_ref, mo_ref, h_ref, wo_ref, g_ref, wr_ref, h2_ref, hn_ref, route_ref, cnt_sc, wob_sc, *,
                n_experts):
    @pl.when(pl.program_id(0) == 0)
    def _():
        cnt_sc[...] = jnp.zeros_like(cnt_sc)

    _cast_weight_once(wo_ref, wob_sc)
    tw = tok_ref.shape[1]
    y = _dot(tok_ref[...], wob_sc[0:tw, :]) + _dot(mo_ref[...], wob_sc[tw:, :])
    h2 = h_ref[...] + y
    h2_ref[...] = h2
    hn = _rms(h2, g_ref[...])
    hn_ref[...] = hn
    _, gidx = _router_gates(_dot(hn.astype(BF16), wr_ref[...]), n_experts)

    tm = h2.shape[0]
    lane = lax.broadcasted_iota(jnp.int32, (tm, LANES), 1)
    onehot = jnp.where(lane.astype(F32) == gidx, 1.0, 0.0)
    earlier = lax.broadcasted_iota(jnp.int32, (tm, tm), 1) < lax.broadcasted_iota(jnp.int32, (tm, tm), 0)
    before = _dot(jnp.where(earlier, 1.0, 0.0).astype(BF16), onehot.astype(BF16)) + cnt_sc[...]
    rank = jnp.sum(onehot * before, axis=-1, keepdims=True)
    cnt_sc[...] += jnp.sum(onehot, axis=0, keepdims=True)
    route_ref[...] = jnp.where(lane == 0, gidx, jnp.where(lane == 1, rank, 0.0))


def _out_proj(tok, mo, h, wo_all, layer, g, wr_bf16, n_experts, tm):
    t, d = h.shape
    tw, mw = tok.shape[1], mo.shape[1]
    row = lambda i: (i, 0)
    const = lambda i: (0, 0)
    return pl.pallas_call(
        functools.partial(_out_kernel, n_experts=n_experts),
        grid=(t // tm,),
        in_specs=[pl.BlockSpec((tm, tw), row), pl.BlockSpec((tm, mw), row), pl.BlockSpec((tm, d), row),
                  pl.BlockSpec((1, d, d), lambda i: (layer, 0, 0), pipeline_mode=pl.Buffered(1)),
                  pl.BlockSpec((1, d), const), pl.BlockSpec((d, LANES), const)],
        out_specs=[pl.BlockSpec((tm, d), row), pl.BlockSpec((tm, d), row), pl.BlockSpec((tm, LANES), row)],
        out_shape=[jax.ShapeDtypeStruct((t, d), F32), jax.ShapeDtypeStruct((t, d), F32),
                   jax.ShapeDtypeStruct((t, LANES), F32)],
        scratch_shapes=[pltpu.VMEM((1, LANES), F32), pltpu.VMEM((d, d), BF16)],
        compiler_params=pltpu.CompilerParams(dimension_semantics=("arbitrary",)),
        name="out_proj_router",
    )(tok, mo, h, wo_all, g, wr_bf16)


def _route_tables(route, tile, n_tiles):
    gidx = route[:, 0].astype(jnp.int32)
    rank = route[:, 1].astype(jnp.int32)
    onehot = gidx[:, None] == jnp.arange(N_GROUPS, dtype=jnp.int32)[None, :]
    cnt = jnp.sum(onehot, axis=0, dtype=jnp.int32)
    ptiles = (cnt + tile - 1) // tile
    tile_end = jnp.cumsum(ptiles)
    off = (tile_end - ptiles) * tile
    pos = jnp.sum(jnp.where(onehot, off[None, :], 0), axis=1) + rank
    tiles = jnp.arange(n_tiles, dtype=jnp.int32)
    tile_group = jnp.minimum(jnp.sum(tiles[:, None] >= tile_end[None, :], axis=1), N_GROUPS - 1)
    return pos.astype(jnp.int32), tile_group.astype(jnp.int32), tile_end[-1:].astype(jnp.int32)


def _sort_kernel(pos_ref, x_ref, xs_in_ref, xs_ref, sem):
    del xs_in_ref

    def make_copy(r):
        return pltpu.make_async_copy(x_ref.at[pl.ds(r, 1)], xs_ref.at[pl.ds(pos_ref[0, 0, r], 1)], sem)

    n = x_ref.shape[0]
    _start_row_copies(n, make_copy)
    pltpu.make_async_copy(x_ref, xs_ref.at[pl.ds(0, n)], sem).wait()


def _sort_rows(x, pos, n_rows, tm):
    t, d = x.shape
    return pl.pallas_call(
        _sort_kernel,
        grid=(t // tm,),
        in_specs=[pl.BlockSpec((1, 1, tm), lambda i: (i, 0, 0), memory_space=pltpu.SMEM),
                  pl.BlockSpec((tm, d), lambda i: (i, 0)),
                  pl.BlockSpec(memory_space=pl.ANY)],
        out_specs=pl.BlockSpec(memory_space=pl.ANY),
        out_shape=jax.ShapeDtypeStruct((n_rows, d), x.dtype),
        input_output_aliases={2: 0},
        scratch_shapes=[pltpu.SemaphoreType.DMA],
        compiler_params=pltpu.CompilerParams(dimension_semantics=("arbitrary",)),
        name="moe_sort_rows",
    )(pos.reshape(t // tm, 1, tm), x, jnp.zeros((n_rows, d), x.dtype))


def _experts_kernel(tg_ref, na_ref, xs_ref, wr_ref, w1_ref, w3_ref, w2_ref, ys_ref, w13b_sc, w2b_sc, *,
                    n_experts):
    i = pl.program_id(0)
    f = w2_ref.shape[2]

    @pl.when((i == 0) | (tg_ref[i] != tg_ref[jnp.maximum(i - 1, 0)]))
    def _():
        for j in range(EXPERTS_PER_GROUP):
            w13b_sc[j, :, 0:f] = w1_ref[0, j].astype(BF16)
            w13b_sc[j, :, f:2 * f] = w3_ref[0, j].astype(BF16)
            w2b_sc[j * f:(j + 1) * f, :] = w2_ref[0, j].astype(BF16)

    @pl.when(i >= na_ref[0])
    def _():
        ys_ref[...] = jnp.zeros_like(ys_ref)

    @pl.when(i < na_ref[0])
    def _():
        x = xs_ref[...].astype(BF16)
        gates, _ = _router_gates(_dot(x, wr_ref[...]), n_experts)
        lane = lax.broadcasted_iota(jnp.int32, gates.shape, 1)
        first = tg_ref[i] * EXPERTS_PER_GROUP
        hid = []
        for j in range(EXPERTS_PER_GROUP):
            au = _dot(x, w13b_sc[j])
            a = au[:, :f]
            u = au[:, f:]
            gate = jnp.sum(jnp.where(lane == first + j, gates, 0.0), axis=-1, keepdims=True)
            hid.append(((a * jax.nn.sigmoid(a)) * u * gate).astype(BF16))
        ys_ref[...] = _dot(jnp.concatenate(hid, axis=-1), w2b_sc[...])


def _experts(xs, tile_group, n_active, wr, w1_all, w3_all, w2_all, layer, n_experts, tile):
    rows, d = xs.shape
    f = w2_all.shape[2]
    epg = EXPERTS_PER_GROUP
    group_w = lambda i, tg, na: (layer, tg[i], 0, 0)
    grid_spec = pltpu.PrefetchScalarGridSpec(
        num_scalar_prefetch=2,
        grid=(rows // tile,),
        in_specs=[pl.BlockSpec((tile, d), lambda i, tg, na: (i, 0)),
                  pl.BlockSpec((d, LANES), lambda i, tg, na: (0, 0)),
                  pl.BlockSpec((1, epg, d, f), group_w),
                  pl.BlockSpec((1, epg, d, f), group_w),
                  pl.BlockSpec((1, epg, f, d), group_w)],
        out_specs=pl.BlockSpec((tile, d), lambda i, tg, na: (i, 0)),
        scratch_shapes=[pltpu.VMEM((epg, d, 2 * f), BF16), pltpu.VMEM((epg * f, d), BF16)],
    )
    return pl.pallas_call(
        functools.partial(_experts_kernel, n_experts=n_experts),
        grid_spec=grid_spec,
        out_shape=jax.ShapeDtypeStruct((rows, d), F32),
        compiler_params=pltpu.CompilerParams(dimension_semantics=("arbitrary",)),
        name="moe_experts",
    )(tile_group, n_active, xs, wr, w1_all, w3_all, w2_all)


def _final_kernel(pos_ref, h2_ref, ys_ref, g_ref, o_ref, ybuf, sem):
    _gather_rows(pos_ref, ys_ref, ybuf, sem)
    o_ref[...] = _rms(h2_ref[...] + ybuf[...], g_ref[...])


def _final_norm(h2, ys, pos, g, tm):
    t, d = h2.shape
    return pl.pallas_call(
        _final_kernel,
        grid=(t // tm,),
        in_specs=[pl.BlockSpec((1, 1, tm), lambda i: (i, 0, 0), memory_space=pltpu.SMEM),
                  pl.BlockSpec((tm, d), lambda i: (i, 0)),
                  pl.BlockSpec(memory_space=pl.ANY),
                  pl.BlockSpec((1, d), lambda i: (0, 0))],
        out_specs=pl.BlockSpec((tm, d), lambda i: (i, 0)),
        out_shape=jax.ShapeDtypeStruct((t, d), F32),
        scratch_shapes=[pltpu.VMEM((tm, d), F32), pltpu.SemaphoreType.DMA],
        name="final_norm_moe_gather",
    )(pos.reshape(t // tm, 1, tm), h2, ys, g)


def kernel(x, mem, norm_mix_g, w_in_a, w_in_b, w_out, conv_w, conv_b, gate_a_w, gate_a_b, gate_x_w, gate_x_b, lru_lambda, diff_lq1, diff_lk1, diff_lq2, diff_lk2, diff_subln_g, rel_bias, mem_norm_g, w_mem_kv, norm_ffn_g, w_group, w_router, w_exp1, w_exp3, w_exp2, final_norm_g):
    batch, seq, d = x.shape
    depth = w_out.shape[0]
    n_experts = w_router.shape[-1]
    t = batch * seq
    rnn_width = conv_w.shape[-1]
    tok_width = d - MEM_WIDTH
    tm = min(512, seq)
    ts = min(256, seq)
    attn_tile = min(512, seq)
    moe_tile = min(512, seq)
    moe_tiles = t // moe_tile + N_GROUPS

    kv = _memkv(mem.reshape(-1, d), mem_norm_g.reshape(1, d), w_mem_kv)
    kv = kv.reshape(depth, batch, mem.shape[1], 2 * MEM_WIDTH)
    bias = _bias_tiles(rel_bias, attn_tile)

    h = x.reshape(t, d)
    moe = None
    for i in range(depth):
        j = i // 2
        g_mix = norm_mix_g[i].reshape(1, d)
        if i % 2 == 0:
            outs = _in_proj(h, moe, g_mix, w_in_a, j, kv, i, batch,
                            (rnn_width, rnn_width), (1.0, 1.0), (F32, F32), tm)
            if moe is not None:
                h, outs = outs[0], outs[1:]
            ux, ug, mo = outs
            tok = _rglru(ux, ug, conv_w[j], conv_b[j].reshape(1, -1),
                         _block_diag(gate_a_w[j]).astype(BF16), gate_a_b[j].reshape(1, -1),
                         _block_diag(gate_x_w[j]).astype(BF16), gate_x_b[j].reshape(1, -1),
                         lru_lambda[j].reshape(1, -1), batch, ts)
        else:
            outs = _in_proj(h, moe, g_mix, w_in_b, j, kv, i, batch,
                            (tok_width, tok_width, tok_width),
                            (DIFF_QK_DIM ** -0.5 * LOG2E, 1.0, 1.0), (BF16, BF16, BF16), tm)
            if moe is not None:
                h, outs = outs[0], outs[1:]
            q, k, v, mo = outs
            lambda_init = 0.8 - 0.6 * math.exp(-0.3 * i)
            shp = (batch, seq, tok_width)
            tok = _diff_attention(q.reshape(shp), k.reshape(shp), v.reshape(shp), bias,
                                  diff_lq1[j].reshape(1, -1), diff_lk1[j].reshape(1, -1),
                                  diff_lq2[j].reshape(1, -1), diff_lk2[j].reshape(1, -1),
                                  diff_subln_g[j].reshape(1, -1), lambda_init, attn_tile)
            tok = tok.reshape(t, tok_width)
        wr = jnp.concatenate([w_router[i], w_group[i]], axis=-1)
        wr = jnp.pad(wr, ((0, 0), (0, LANES - wr.shape[1]))).astype(BF16)
        h, hn, route = _out_proj(tok, mo, h, w_out, i, norm_ffn_g[i].reshape(1, d), wr, n_experts, tm)
        pos, tile_group, n_active = _route_tables(route, moe_tile, moe_tiles)
        xs = _sort_rows(hn, pos, moe_tiles * moe_tile, tm)
        ys = _experts(xs, tile_group, n_active, wr, w_exp1, w_exp3, w_exp2, i, n_experts, moe_tile)
        moe = (ys, pos)
    return _final_norm(h, moe[0], moe[1], final_norm_g.reshape(1, d), tm).reshape(batch, seq, d)
```

```python
import functools
import math

import numpy as np
import jax
import jax.numpy as jnp
from jax import lax
from jax.experimental import pallas as pl
from jax.experimental.pallas import tpu as pltpu

F32 = jnp.float32
BF16 = jnp.bfloat16
EPS = 1e-6
NEG = -1e30
LOG2E = math.log2(math.e)

MEM_HEADS = 4
MEM_HEAD_DIM = 64
MEM_WIDTH = MEM_HEADS * MEM_HEAD_DIM
CONV_WIDTH = 4
LRU_C = 8.0
DIFF_QK_DIM = 64
DIFF_V_DIM = 2 * DIFF_QK_DIM
REL_BUCKETS = 32
REL_MAX_EXACT = 16
REL_MAX_DIST = 128
N_GROUPS = 4
EXPERTS_PER_GROUP = 4
LANES = 128
SUBLANES = 8


def _rms(x, g):
    return x * lax.rsqrt(jnp.mean(x * x, axis=-1, keepdims=True) + EPS) * g


def _dot(a, b):
    return jnp.dot(a, b, preferred_element_type=F32)


def _dot_nt(a, b):
    return lax.dot_general(a, b, (((1,), (1,)), ((), ())), preferred_element_type=F32)


def _memkv_kernel(mem_ref, g_ref, w_ref, o_ref):
    xn = _rms(mem_ref[...], g_ref[...]).astype(BF16)
    o_ref[0] = _dot(xn, w_ref[0].astype(BF16)).astype(BF16)


def _memkv(mem2d, g, w):
    depth, d, n = w.shape
    rows = mem2d.shape[0]
    return pl.pallas_call(
        _memkv_kernel,
        grid=(depth,),
        in_specs=[pl.BlockSpec((rows, d), lambda i: (0, 0)),
                  pl.BlockSpec((1, d), lambda i: (0, 0)),
                  pl.BlockSpec((1, d, n), lambda i: (i, 0, 0))],
        out_specs=pl.BlockSpec((1, rows, n), lambda i: (i, 0, 0)),
        out_shape=jax.ShapeDtypeStruct((depth, rows, n), BF16),
        name="memkv",
    )(mem2d, g, w)


def _to_token_tiles(ref, x):
    n = x.shape[0]
    for c in range(SUBLANES):
        ref[pl.ds(c, n, stride=SUBLANES), :] = x[:, c * LANES:(c + 1) * LANES]


def _from_token_tiles(ref):
    n = ref.shape[0] // SUBLANES
    return jnp.concatenate([ref[pl.ds(c, n, stride=SUBLANES), :] for c in range(SUBLANES)], axis=1)


def _token_tile(ref, t):
    return ref.at[pl.ds(pl.multiple_of(t * SUBLANES, SUBLANES), SUBLANES)]


def _start_row_copies(n, make_copy):
    def body(p, carry):
        for u in range(2):
            make_copy(2 * p + u).start(priority=u)
        return carry

    lax.fori_loop(0, n // 2, body, 0, unroll=4)


def _gather_rows(pos_ref, src_hbm, dst_vmem, sem):
    def make_copy(r):
        return pltpu.make_async_copy(_token_tile(src_hbm, pos_ref[0, 0, r]), _token_tile(dst_vmem, r), sem)

    _start_row_copies(dst_vmem.shape[0] // SUBLANES, make_copy)
    pltpu.make_async_copy(src_hbm.at[pl.ds(0, dst_vmem.shape[0])], dst_vmem, sem).wait()


def _mem_attn(qm, kv):
    outs = []
    for hh in range(MEM_HEADS):
        lo, hi = hh * MEM_HEAD_DIM, (hh + 1) * MEM_HEAD_DIM
        q = (qm[:, lo:hi] * (MEM_HEAD_DIM ** -0.5)).astype(BF16)
        k = kv[:, lo:hi]
        v = kv[:, MEM_WIDTH + lo:MEM_WIDTH + hi]
        s = _dot_nt(q, k)
        m = jnp.max(s, axis=-1, keepdims=True)
        p = jnp.exp(s - m)
        l = jnp.sum(p, axis=-1, keepdims=True)
        outs.append(_dot(p.astype(BF16), v) / l)
    return jnp.concatenate(outs, axis=-1)


def _cast_weight_once(w_ref, wb_sc):
    @pl.when(pl.program_id(0) == 0)
    def _():
        wb_sc[...] = w_ref[0].astype(BF16)


def _project(h, g_ref, wb_sc, kv_ref, out_refs, splits, scales):
    hn = _rms(h, g_ref[...]).astype(BF16)
    proj = _dot(hn, wb_sc[...])
    off = 0
    for ref, width, scale in zip(out_refs[:-1], splits, scales):
        piece = proj[:, off:off + width]
        if scale != 1.0:
            piece = piece * scale
        ref[...] = piece.astype(ref.dtype)
        off += width
    out_refs[-1][...] = _mem_attn(proj[:, off:off + MEM_WIDTH], kv_ref[0, 0]).astype(BF16)


def _in_kernel(h_ref, g_ref, w_ref, kv_ref, *rest, splits, scales):
    out_refs, wb_sc = rest[:-1], rest[-1]
    _cast_weight_once(w_ref, wb_sc)
    _project(h_ref[...], g_ref, wb_sc, kv_ref, out_refs, splits, scales)


def _in_moe_kernel(pos_ref, h2_ref, ys_ref, g_ref, w_ref, kv_ref, h_ref, *rest, splits, scales):
    out_refs, (ybuf, sem, wb_sc) = rest[:-3], rest[-3:]
    _cast_weight_once(w_ref, wb_sc)
    _gather_rows(pos_ref, ys_ref, ybuf, sem)
    h = h2_ref[...] + _from_token_tiles(ybuf)
    h_ref[...] = h
    _project(h, g_ref, wb_sc, kv_ref, out_refs, splits, scales)


def _in_proj(h, moe, g, w_all, w_idx, kv, layer, batch, splits, scales, dtypes, tm):
    t, d = h.shape
    n = w_all.shape[2]
    tiles_per_batch = (t // batch) // tm
    mem_len = kv.shape[2]
    row = lambda i: (i, 0)
    const = lambda i: (0, 0)
    out_shape = [jax.ShapeDtypeStruct((t, w), dt) for w, dt in zip(splits, dtypes)]
    out_shape.append(jax.ShapeDtypeStruct((t, MEM_WIDTH), BF16))
    out_specs = [pl.BlockSpec((tm, w), row) for w in splits]
    out_specs.append(pl.BlockSpec((tm, MEM_WIDTH), row))
    in_specs = [pl.BlockSpec((tm, d), row),
                pl.BlockSpec((1, d), const),
                pl.BlockSpec((1, d, n), lambda i: (w_idx, 0, 0), pipeline_mode=pl.Buffered(1)),
                pl.BlockSpec((1, 1, mem_len, 2 * MEM_WIDTH),
                             lambda i: (layer, i // tiles_per_batch, 0, 0))]
    if moe is None:
        return pl.pallas_call(
            functools.partial(_in_kernel, splits=splits, scales=scales),
            grid=(t // tm,), in_specs=in_specs, out_specs=out_specs, out_shape=out_shape,
            scratch_shapes=[pltpu.VMEM((d, n), BF16)],
            compiler_params=pltpu.CompilerParams(dimension_semantics=("arbitrary",)),
            name="in_proj",
        )(h, g, w_all, kv)
    ys, pos = moe
    in_specs = ([pl.BlockSpec((1, 1, tm), lambda i: (i, 0, 0), memory_space=pltpu.SMEM), in_specs[0],
                 pl.BlockSpec(memory_space=pl.ANY)] + in_specs[1:])
    return pl.pallas_call(
        functools.partial(_in_moe_kernel, splits=splits, scales=scales),
        grid=(t // tm,), in_specs=in_specs,
        out_specs=[pl.BlockSpec((tm, d), row)] + out_specs,
        out_shape=[jax.ShapeDtypeStruct((t, d), F32)] + out_shape,
        scratch_shapes=[pltpu.VMEM((tm * SUBLANES, LANES), F32), pltpu.SemaphoreType.DMA,
                        pltpu.VMEM((d, n), BF16)],
        compiler_params=pltpu.CompilerParams(dimension_semantics=("arbitrary",)),
        name="in_proj_moe_gather",
    )(pos.reshape(t // tm, 1, tm), h, ys, g, w_all, kv)


def _gelu_tanh(x):
    return 0.5 * x * (1.0 + jnp.tanh(math.sqrt(2.0 / math.pi) * (x + 0.044715 * (x * x * x))))


def _softplus(z):
    return jnp.maximum(z, 0.0) + jnp.log1p(jnp.exp(-jnp.abs(z)))


def _rglru_kernel(ux_ref, ug_ref, cw_ref, cb_ref, wa_ref, ba_ref, wx_ref, bx_ref, lam_ref,
                  o_ref, xpad_sc, h_sc):
    ts, c = ux_ref.shape
    pad = 8

    @pl.when(pl.program_id(1) == 0)
    def _():
        xpad_sc[0:pad, :] = jnp.zeros((pad, c), F32)
        h_sc[...] = jnp.zeros_like(h_sc)

    ux = ux_ref[...]
    xpad_sc[pad:pad + ts, :] = ux
    xc = cb_ref[...] + cw_ref[CONV_WIDTH - 1:CONV_WIDTH, :] * ux
    for j in range(CONV_WIDTH - 1):
        xc = xc + cw_ref[j:j + 1, :] * xpad_sc[pl.ds(pad - (CONV_WIDTH - 1) + j, ts), :]
    xpad_sc[0:pad, :] = ux[ts - pad:ts, :]

    xcb = xc.astype(BF16)
    r = jax.nn.sigmoid(_dot(xcb, wa_ref[...]) + ba_ref[...])
    ig = jax.nn.sigmoid(_dot(xcb, wx_ref[...]) + bx_ref[...])
    log_a = (-LRU_C) * r * _softplus(-lam_ref[...])
    a = jnp.exp(log_a)
    b = jnp.sqrt(-jnp.tanh(log_a) * (a * a + 1.0)) * (ig * xc)

    row = lax.broadcasted_iota(jnp.int32, (ts, c), 0) & (SUBLANES - 1)
    k = 1
    while k < SUBLANES:
        keep = row >= k
        a_s = jnp.where(keep, pltpu.roll(a, k, 0), 1.0)
        b_s = jnp.where(keep, pltpu.roll(b, k, 0), 0.0)
        b = a * b_s + b
        a = a * a_s
        k *= 2
    h_in = h_sc[...]
    groups = []
    for g in range(ts // SUBLANES):
        rows = slice(g * SUBLANES, (g + 1) * SUBLANES)
        groups.append(b[rows] + a[rows] * h_in)
        h_in = groups[-1][SUBLANES - 1:SUBLANES, :]
    h_sc[...] = h_in
    o_ref[...] = (jnp.concatenate(groups, axis=0) * _gelu_tanh(ug_ref[...])).astype(o_ref.dtype)


def _rglru(ux, ug, cw, cb, wa, ba, wx, bx, lam, batch, ts):
    t, c = ux.shape
    nt = (t // batch) // ts
    row = lambda b, s: (b * nt + s, 0)
    const = lambda b, s: (0, 0)
    return pl.pallas_call(
        _rglru_kernel,
        grid=(batch, nt),
        in_specs=[pl.BlockSpec((ts, c), row), pl.BlockSpec((ts, c), row),
                  pl.BlockSpec((CONV_WIDTH, c), const), pl.BlockSpec((1, c), const),
                  pl.BlockSpec((c, c), const), pl.BlockSpec((1, c), const),
                  pl.BlockSpec((c, c), const), pl.BlockSpec((1, c), const),
                  pl.BlockSpec((1, c), const)],
        out_specs=pl.BlockSpec((ts, c), row),
        out_shape=jax.ShapeDtypeStruct((t, c), BF16),
        scratch_shapes=[pltpu.VMEM((ts + 8, c), F32), pltpu.VMEM((1, c), F32)],
        compiler_params=pltpu.CompilerParams(dimension_semantics=("arbitrary", "arbitrary")),
        name="rglru",
    )(ux, ug, cw, cb, wa, ba, wx, bx, lam)


def _block_diag(w):
    g, n, _ = w.shape
    eye = jnp.eye(g, dtype=w.dtype)
    return (eye[:, None, :, None] * w[:, :, None, :]).reshape(g * n, g * n)


def _bucket_table(tile):
    i = np.arange(tile)[:, None]
    j = np.arange(tile)[None, :]

    def bucket(n):
        nf = np.maximum(n, 1).astype(np.float32)
        large = REL_MAX_EXACT + (np.log(nf / REL_MAX_EXACT) / math.log(REL_MAX_DIST / REL_MAX_EXACT)
                                 * (REL_BUCKETS - REL_MAX_EXACT)).astype(np.int32)
        return np.where(n < REL_MAX_EXACT, n, np.minimum(large, REL_BUCKETS - 1))

    diag = np.where(j <= i, bucket(np.maximum(i - j, 0)), -1)
    left = bucket(tile + i - j)
    return np.stack([diag, left]).astype(np.int32)


def _bias_kernel(rb_ref, bucket_ref, o_ref):
    h = pl.program_id(0)
    bk = bucket_ref[...]
    far = rb_ref[REL_BUCKETS - 1, h]
    acc = jnp.zeros(bk.shape, F32)
    for b in range(REL_BUCKETS - 1):
        acc = jnp.where(bk == b, rb_ref[b, h] - far, acc)
    o_ref[0] = jnp.where(bk < 0, NEG, acc * LOG2E)


def _bias_tiles(rel_bias, tile):
    assert tile >= REL_MAX_DIST
    heads = rel_bias.shape[1]
    buckets = jnp.asarray(_bucket_table(tile))
    return pl.pallas_call(
        _bias_kernel,
        grid=(heads,),
        in_specs=[pl.BlockSpec(memory_space=pltpu.SMEM),
                  pl.BlockSpec((2, tile, tile), lambda h: (0, 0, 0))],
        out_specs=pl.BlockSpec((1, 2, tile, tile), lambda h: (h, 0, 0, 0)),
        out_shape=jax.ShapeDtypeStruct((heads, 2, tile, tile), F32),
        name="rel_bias_tiles",
    )(rel_bias, buckets)


def _attn_kernel(q_ref, k_ref, v_ref, bias_ref, lq1_ref, lk1_ref, lq2_ref, lk2_ref,
                 g_ref, o_ref, m_sc, acc_sc, sa_sc, sb_sc, *, lambda_init):
    qi = pl.program_id(2)
    tile = q_ref.shape[1]
    reps = tile // LANES
    m_sc[...] = jnp.full(m_sc.shape, NEG, F32)
    acc_sc[...] = jnp.zeros_like(acc_sc)

    def scores(ki, s_ref):
        start = pl.multiple_of(ki * tile, tile)
        q = q_ref[0]
        k = k_ref[0, pl.ds(start, tile), :]
        for j in range(2):
            lo, hi = j * DIFF_QK_DIM, (j + 1) * DIFF_QK_DIM
            s_ref[j] = _dot_nt(q[:, lo:hi], k[:, lo:hi])

    def accumulate(ki, s_ref, bias_slot):
        start = pl.multiple_of(ki * tile, tile)
        v = v_ref[0, pl.ds(start, tile), :]
        v1 = jnp.concatenate([v, jnp.ones_like(v)], axis=-1)
        for j in range(2):
            s = s_ref[j]
            if bias_slot is not None:
                s = s + bias_ref[0, bias_slot]
            m_prev = m_sc[j]
            m_new = jnp.maximum(m_prev, jnp.max(s, axis=-1, keepdims=True))
            alpha = jnp.exp2(m_prev - m_new)
            p = jnp.exp2(s - jnp.concatenate([m_new] * reps, axis=1))
            acc_sc[j] = jnp.concatenate([alpha, alpha], axis=1) * acc_sc[j] + _dot(p.astype(BF16), v1)
            m_sc[j] = m_new

    n_far = jnp.maximum(qi - 1, 0)
    odd = n_far & 1
    pl.when(odd == 0)(lambda: scores(0, sa_sc))

    @pl.when(odd == 1)
    def _():
        scores(0, sb_sc)
        scores(1, sa_sc)
        accumulate(0, sb_sc, None)

    def far_pair(p, carry):
        base = odd + 2 * p
        scores(base + 1, sb_sc)
        accumulate(base, sa_sc, None)
        scores(base + 2, sa_sc)
        accumulate(base + 1, sb_sc, None)
        return carry

    lax.fori_loop(0, n_far >> 1, far_pair, 0)

    @pl.when(qi >= 1)
    def _():
        scores(qi, sb_sc)
        accumulate(qi - 1, sa_sc, 1)
        accumulate(qi, sb_sc, 0)

    @pl.when(qi == 0)
    def _():
        accumulate(0, sa_sc, 0)

    lam = (jnp.exp(jnp.sum(lq1_ref[...] * lk1_ref[...], axis=-1, keepdims=True))
           - jnp.exp(jnp.sum(lq2_ref[...] * lk2_ref[...], axis=-1, keepdims=True))
           + lambda_init)
    dv = o_ref.shape[-1]
    o = (acc_sc[0, :, :dv] / acc_sc[0, :, dv:]) - lam * (acc_sc[1, :, :dv] / acc_sc[1, :, dv:])
    o_ref[0] = (_rms(o, g_ref[...]) * (1.0 - lambda_init)).astype(o_ref.dtype)


def _diff_attention(q, k, v, bias, lq1, lk1, lq2, lk2, g, lambda_init, tile):
    batch, s, width = q.shape
    heads = width // DIFF_V_DIM
    vec = lambda b, h, i: (0, 0)
    head_all = lambda b, h, i: (b, 0, h)
    return pl.pallas_call(
        functools.partial(_attn_kernel, lambda_init=lambda_init),
        grid=(batch, heads, s // tile),
        in_specs=[pl.BlockSpec((1, tile, DIFF_V_DIM), lambda b, h, i: (b, i, h)),
                  pl.BlockSpec((1, s, DIFF_V_DIM), head_all),
                  pl.BlockSpec((1, s, DIFF_V_DIM), head_all),
                  pl.BlockSpec((1, 2, tile, tile), lambda b, h, i: (h, 0, 0, 0)),
                  pl.BlockSpec((1, DIFF_QK_DIM), vec), pl.BlockSpec((1, DIFF_QK_DIM), vec),
                  pl.BlockSpec((1, DIFF_QK_DIM), vec), pl.BlockSpec((1, DIFF_QK_DIM), vec),
                  pl.BlockSpec((1, DIFF_V_DIM), vec)],
        out_specs=pl.BlockSpec((1, tile, DIFF_V_DIM), lambda b, h, i: (b, i, h)),
        out_shape=jax.ShapeDtypeStruct((batch, s, width), BF16),
        scratch_shapes=[pltpu.VMEM((2, tile, LANES), F32),
                        pltpu.VMEM((2, tile, 2 * DIFF_V_DIM), F32),
                        pltpu.VMEM((2, tile, tile), F32), pltpu.VMEM((2, tile, tile), F32)],
        compiler_params=pltpu.CompilerParams(
            dimension_semantics=("arbitrary", "arbitrary", "arbitrary")),
        name="diff_attn",
    )(q, k, v, bias, lq1, lk1, lq2, lk2, g)


def _router_gates(logits, n_experts):
    lane = lax.broadcasted_iota(jnp.int32, logits.shape, 1)
    lane_f = lane.astype(F32)
    is_g = (lane >= n_experts) & (lane < n_experts + N_GROUPS)
    gl = jnp.where(is_g, logits, NEG)
    gmax = jnp.max(gl, axis=-1, keepdims=True)
    gidx = jnp.min(jnp.where(gl == gmax, lane_f - n_experts, 1e9), axis=-1, keepdims=True)
    gsum = jnp.sum(jnp.where(is_g, jnp.exp(gl - gmax), 0.0), axis=-1, keepdims=True)
    g_w = 1.0 / gsum
    in_grp = (lane < n_experts) & ((lane // EXPERTS_PER_GROUP).astype(F32) == gidx)
    el = jnp.where(in_grp, logits, NEG)
    t1 = jnp.max(el, axis=-1, keepdims=True)
    i1 = jnp.min(jnp.where(el == t1, lane_f, 1e9), axis=-1, keepdims=True)
    el2 = jnp.where(lane_f == i1, NEG, el)
    t2 = jnp.max(el2, axis=-1, keepdims=True)
    i2 = jnp.min(jnp.where(el2 == t2, lane_f, 1e9), axis=-1, keepdims=True)
    e2 = jnp.exp(t2 - t1)
    w1 = g_w / (1.0 + e2)
    w2 = g_w * e2 / (1.0 + e2)
    return jnp.where(lane_f == i1, w1, jnp.where(lane_f == i2, w2, 0.0)), gidx


def _out_kernel(tok_ref, mo_ref, h_ref, wo_ref, g_ref, wr_ref, h2_ref, hn_ref, route_ref, cnt_sc, wob_sc, *,
                n_experts):
    @pl.when(pl.program_id(0) == 0)
    def _():
        cnt_sc[...] = jnp.zeros_like(cnt_sc)

    _cast_weight_once(wo_ref, wob_sc)
    tw = tok_ref.shape[1]
    y = _dot(tok_ref[...], wob_sc[0:tw, :]) + _dot(mo_ref[...], wob_sc[tw:, :])
    h2 = h_ref[...] + y
    h2_ref[...] = h2
    hn = _rms(h2, g_ref[...])
    _to_token_tiles(hn_ref, hn)
    _, gidx = _router_gates(_dot(hn.astype(BF16), wr_ref[...]), n_experts)

    tm = h2.shape[0]
    lane = lax.broadcasted_iota(jnp.int32, (tm, LANES), 1)
    onehot = jnp.where(lane.astype(F32) == gidx, 1.0, 0.0)
    earlier = lax.broadcasted_iota(jnp.int32, (tm, tm), 1) < lax.broadcasted_iota(jnp.int32, (tm, tm), 0)
    before = _dot(jnp.where(earlier, 1.0, 0.0).astype(BF16), onehot.astype(BF16)) + cnt_sc[...]
    rank = jnp.sum(onehot * before, axis=-1, keepdims=True)
    cnt_sc[...] += jnp.sum(onehot, axis=0, keepdims=True)
    route_ref[...] = jnp.where(lane == 0, gidx, jnp.where(lane == 1, rank, 0.0))


def _out_proj(tok, mo, h, wo_all, layer, g, wr_bf16, n_experts, tm):
    t, d = h.shape
    tw, mw = tok.shape[1], mo.shape[1]
    row = lambda i: (i, 0)
    const = lambda i: (0, 0)
    return pl.pallas_call(
        functools.partial(_out_kernel, n_experts=n_experts),
        grid=(t // tm,),
        in_specs=[pl.BlockSpec((tm, tw), row), pl.BlockSpec((tm, mw), row), pl.BlockSpec((tm, d), row),
                  pl.BlockSpec((1, d, d), lambda i: (layer, 0, 0), pipeline_mode=pl.Buffered(1)),
                  pl.BlockSpec((1, d), const), pl.BlockSpec((d, LANES), const)],
        out_specs=[pl.BlockSpec((tm, d), row), pl.BlockSpec((tm * SUBLANES, LANES), row),
                   pl.BlockSpec((tm, LANES), row)],
        out_shape=[jax.ShapeDtypeStruct((t, d), F32), jax.ShapeDtypeStruct((t * SUBLANES, LANES), F32),
                   jax.ShapeDtypeStruct((t, LANES), F32)],
        scratch_shapes=[pltpu.VMEM((1, LANES), F32), pltpu.VMEM((d, d), BF16)],
        compiler_params=pltpu.CompilerParams(dimension_semantics=("arbitrary",)),
        name="out_proj_router",
    )(tok, mo, h, wo_all, g, wr_bf16)


def _route_tables(route, tile, n_tiles):
    gidx = route[:, 0].astype(jnp.int32)
    rank = route[:, 1].astype(jnp.int32)
    onehot = gidx[:, None] == jnp.arange(N_GROUPS, dtype=jnp.int32)[None, :]
    cnt = jnp.sum(onehot, axis=0, dtype=jnp.int32)
    ptiles = (cnt + tile - 1) // tile
    tile_end = jnp.cumsum(ptiles)
    off = (tile_end - ptiles) * tile
    pos = jnp.sum(jnp.where(onehot, off[None, :], 0), axis=1) + rank
    tiles = jnp.arange(n_tiles, dtype=jnp.int32)
    tile_group = jnp.minimum(jnp.sum(tiles[:, None] >= tile_end[None, :], axis=1), N_GROUPS - 1)
    return pos.astype(jnp.int32), tile_group.astype(jnp.int32), tile_end[-1:].astype(jnp.int32)


def _sort_kernel(pos_ref, x_ref, xs_in_ref, xs_ref, sem):
    del xs_in_ref

    def make_copy(r):
        return pltpu.make_async_copy(_token_tile(x_ref, r), _token_tile(xs_ref, pos_ref[0, 0, r]), sem)

    n = x_ref.shape[0]
    _start_row_copies(n // SUBLANES, make_copy)
    pltpu.make_async_copy(x_ref, xs_ref.at[pl.ds(0, n)], sem).wait()


def _sort_rows(x, pos, n_tokens_out, tm):
    t = pos.shape[0]
    return pl.pallas_call(
        _sort_kernel,
        grid=(t // tm,),
        in_specs=[pl.BlockSpec((1, 1, tm), lambda i: (i, 0, 0), memory_space=pltpu.SMEM),
                  pl.BlockSpec((tm * SUBLANES, LANES), lambda i: (i, 0)),
                  pl.BlockSpec(memory_space=pl.ANY)],
        out_specs=pl.BlockSpec(memory_space=pl.ANY),
        out_shape=jax.ShapeDtypeStruct((n_tokens_out * SUBLANES, LANES), x.dtype),
        input_output_aliases={2: 0},
        scratch_shapes=[pltpu.SemaphoreType.DMA],
        compiler_params=pltpu.CompilerParams(dimension_semantics=("arbitrary",)),
        name="moe_sort_rows",
    )(pos.reshape(t // tm, 1, tm), x, jnp.zeros((n_tokens_out * SUBLANES, LANES), x.dtype))


def _experts_kernel(tg_ref, na_ref, xs_ref, wr_ref, w1_ref, w3_ref, w2_ref, ys_ref, w13b_sc, w2b_sc, *,
                    n_experts):
    i = pl.program_id(0)
    f = w2_ref.shape[2]

    @pl.when((i == 0) | (tg_ref[i] != tg_ref[jnp.maximum(i - 1, 0)]))
    def _():
        for j in range(EXPERTS_PER_GROUP):
            w13b_sc[j, :, 0:f] = w1_ref[0, j].astype(BF16)
            w13b_sc[j, :, f:2 * f] = w3_ref[0, j].astype(BF16)
            w2b_sc[j * f:(j + 1) * f, :] = w2_ref[0, j].astype(BF16)

    @pl.when(i >= na_ref[0])
    def _():
        ys_ref[...] = jnp.zeros_like(ys_ref)

    @pl.when(i < na_ref[0])
    def _():
        x = _from_token_tiles(xs_ref).astype(BF16)
        gates, _ = _router_gates(_dot(x, wr_ref[...]), n_experts)
        lane = lax.broadcasted_iota(jnp.int32, gates.shape, 1)
        first = tg_ref[i] * EXPERTS_PER_GROUP
        hid = []
        for j in range(EXPERTS_PER_GROUP):
            au = _dot(x, w13b_sc[j])
            a = au[:, :f]
            u = au[:, f:]
            gate = jnp.sum(jnp.where(lane == first + j, gates, 0.0), axis=-1, keepdims=True)
            hid.append(((a * jax.nn.sigmoid(a)) * u * gate).astype(BF16))
        _to_token_tiles(ys_ref, _dot(jnp.concatenate(hid, axis=-1), w2b_sc[...]))


def _experts(xs, tile_group, n_active, wr, w1_all, w3_all, w2_all, layer, n_experts, tile):
    f, d = w2_all.shape[2:]
    rows = xs.shape[0] // SUBLANES
    epg = EXPERTS_PER_GROUP
    group_w = lambda i, tg, na: (layer, tg[i], 0, 0)
    grid_spec = pltpu.PrefetchScalarGridSpec(
        num_scalar_prefetch=2,
        grid=(rows // tile,),
        in_specs=[pl.BlockSpec((tile * SUBLANES, LANES), lambda i, tg, na: (i, 0)),
                  pl.BlockSpec((d, LANES), lambda i, tg, na: (0, 0)),
                  pl.BlockSpec((1, epg, d, f), group_w),
                  pl.BlockSpec((1, epg, d, f), group_w),
                  pl.BlockSpec((1, epg, f, d), group_w)],
        out_specs=pl.BlockSpec((tile * SUBLANES, LANES), lambda i, tg, na: (i, 0)),
        scratch_shapes=[pltpu.VMEM((epg, d, 2 * f), BF16), pltpu.VMEM((epg * f, d), BF16)],
    )
    return pl.pallas_call(
        functools.partial(_experts_kernel, n_experts=n_experts),
        grid_spec=grid_spec,
        out_shape=jax.ShapeDtypeStruct(xs.shape, F32),
        compiler_params=pltpu.CompilerParams(dimension_semantics=("arbitrary",)),
        name="moe_experts",
    )(tile_group, n_active, xs, wr, w1_all, w3_all, w2_all)


def _final_kernel(pos_ref, h2_ref, ys_ref, g_ref, o_ref, ybuf, sem):
    _gather_rows(pos_ref, ys_ref, ybuf, sem)
    o_ref[...] = _rms(h2_ref[...] + _from_token_tiles(ybuf), g_ref[...])


def _final_norm(h2, ys, pos, g, tm):
    t, d = h2.shape
    return pl.pallas_call(
        _final_kernel,
        grid=(t // tm,),
        in_specs=[pl.BlockSpec((1, 1, tm), lambda i: (i, 0, 0), memory_space=pltpu.SMEM),
                  pl.BlockSpec((tm, d), lambda i: (i, 0)),
                  pl.BlockSpec(memory_space=pl.ANY),
                  pl.BlockSpec((1, d), lambda i: (0, 0))],
        out_specs=pl.BlockSpec((tm, d), lambda i: (i, 0)),
        out_shape=jax.ShapeDtypeStruct((t, d), F32),
        scratch_shapes=[pltpu.VMEM((tm * SUBLANES, LANES), F32), pltpu.SemaphoreType.DMA],
        name="final_norm_moe_gather",
    )(pos.reshape(t // tm, 1, tm), h2, ys, g)


def kernel(x, mem, norm_mix_g, w_in_a, w_in_b, w_out, conv_w, conv_b, gate_a_w, gate_a_b, gate_x_w, gate_x_b, lru_lambda, diff_lq1, diff_lk1, diff_lq2, diff_lk2, diff_subln_g, rel_bias, mem_norm_g, w_mem_kv, norm_ffn_g, w_group, w_router, w_exp1, w_exp3, w_exp2, final_norm_g):
    batch, seq, d = x.shape
    depth = w_out.shape[0]
    n_experts = w_router.shape[-1]
    t = batch * seq
    rnn_width = conv_w.shape[-1]
    tok_width = d - MEM_WIDTH
    tm = min(512, seq)
    ts = min(256, seq)
    attn_tile = min(512, seq)
    moe_tile = min(512, seq)
    moe_tiles = t // moe_tile + N_GROUPS

    kv = _memkv(mem.reshape(-1, d), mem_norm_g.reshape(1, d), w_mem_kv)
    kv = kv.reshape(depth, batch, mem.shape[1], 2 * MEM_WIDTH)
    bias = _bias_tiles(rel_bias, attn_tile)

    h = x.reshape(t, d)
    moe = None
    for i in range(depth):
        j = i // 2
        g_mix = norm_mix_g[i].reshape(1, d)
        if i % 2 == 0:
            outs = _in_proj(h, moe, g_mix, w_in_a, j, kv, i, batch,
                            (rnn_width, rnn_width), (1.0, 1.0), (F32, F32), tm)
            if moe is not None:
                h, outs = outs[0], outs[1:]
            ux, ug, mo = outs
            tok = _rglru(ux, ug, conv_w[j], conv_b[j].reshape(1, -1),
                         _block_diag(gate_a_w[j]).astype(BF16), gate_a_b[j].reshape(1, -1),
                         _block_diag(gate_x_w[j]).astype(BF16), gate_x_b[j].reshape(1, -1),
                         lru_lambda[j].reshape(1, -1), batch, ts)
        else:
            outs = _in_proj(h, moe, g_mix, w_in_b, j, kv, i, batch,
                            (tok_width, tok_width, tok_width),
                            (DIFF_QK_DIM ** -0.5 * LOG2E, 1.0, 1.0), (BF16, BF16, BF16), tm)
            if moe is not None:
                h, outs = outs[0], outs[1:]
            q, k, v, mo = outs
            lambda_init = 0.8 - 0.6 * math.exp(-0.3 * i)
            shp = (batch, seq, tok_width)
            tok = _diff_attention(q.reshape(shp), k.reshape(shp), v.reshape(shp), bias,
                                  diff_lq1[j].reshape(1, -1), diff_lk1[j].reshape(1, -1),
                                  diff_lq2[j].reshape(1, -1), diff_lk2[j].reshape(1, -1),
                                  diff_subln_g[j].reshape(1, -1), lambda_init, attn_tile)
            tok = tok.reshape(t, tok_width)
        wr = jnp.concatenate([w_router[i], w_group[i]], axis=-1)
        wr = jnp.pad(wr, ((0, 0), (0, LANES - wr.shape[1]))).astype(BF16)
        h, hn, route = _out_proj(tok, mo, h, w_out, i, norm_ffn_g[i].reshape(1, d), wr, n_experts, tm)
        pos, tile_group, n_active = _route_tables(route, moe_tile, moe_tiles)
        assert d == SUBLANES * LANES
        xs = _sort_rows(hn, pos, moe_tiles * moe_tile, tm)
        ys = _experts(xs, tile_group, n_active, wr, w_exp1, w_exp3, w_exp2, i, n_experts, moe_tile)
        moe = (ys, pos)
    return _final_norm(h, moe[0], moe[1], final_norm_g.reshape(1, d), tm).reshape(batch, seq, d)
```

```python
import functools
import math

import numpy as np
import jax
import jax.numpy as jnp
from jax import lax
from jax.experimental import pallas as pl
from jax.experimental.pallas import tpu as pltpu

F32 = jnp.float32
BF16 = jnp.bfloat16
EPS = 1e-6
NEG = -1e30
LOG2E = math.log2(math.e)

MEM_HEADS = 4
MEM_HEAD_DIM = 64
MEM_WIDTH = MEM_HEADS * MEM_HEAD_DIM
CONV_WIDTH = 4
LRU_C = 8.0
DIFF_QK_DIM = 64
DIFF_V_DIM = 2 * DIFF_QK_DIM
REL_BUCKETS = 32
REL_MAX_EXACT = 16
REL_MAX_DIST = 128
N_GROUPS = 4
EXPERTS_PER_GROUP = 4
LANES = 128
SUBLANES = 8


def _rms(x, g):
    return x * lax.rsqrt(jnp.mean(x * x, axis=-1, keepdims=True) + EPS) * g


def _dot(a, b):
    return jnp.dot(a, b, preferred_element_type=F32)


def _dot_nt(a, b):
    return lax.dot_general(a, b, (((1,), (1,)), ((), ())), preferred_element_type=F32)


def _memkv_kernel(mem_ref, g_ref, w_ref, o_ref):
    xn = _rms(mem_ref[...], g_ref[...]).astype(BF16)
    o_ref[0] = _dot(xn, w_ref[0].astype(BF16)).astype(BF16)


def _memkv(mem2d, g, w):
    depth, d, n = w.shape
    rows = mem2d.shape[0]
    return pl.pallas_call(
        _memkv_kernel,
        grid=(depth,),
        in_specs=[pl.BlockSpec((rows, d), lambda i: (0, 0)),
                  pl.BlockSpec((1, d), lambda i: (0, 0)),
                  pl.BlockSpec((1, d, n), lambda i: (i, 0, 0))],
        out_specs=pl.BlockSpec((1, rows, n), lambda i: (i, 0, 0)),
        out_shape=jax.ShapeDtypeStruct((depth, rows, n), BF16),
        name="memkv",
    )(mem2d, g, w)


def _to_token_tiles(ref, x):
    n = x.shape[0]
    for c in range(SUBLANES):
        ref[pl.ds(c, n, stride=SUBLANES), :] = x[:, c * LANES:(c + 1) * LANES]


def _from_token_tiles(ref):
    n = ref.shape[0] // SUBLANES
    return jnp.concatenate([ref[pl.ds(c, n, stride=SUBLANES), :] for c in range(SUBLANES)], axis=1)


def _token_tile(ref, t):
    return ref.at[pl.ds(pl.multiple_of(t * SUBLANES, SUBLANES), SUBLANES)]


def _start_row_copies(n, make_copy):
    def body(p, carry):
        for u in range(2):
            make_copy(2 * p + u).start(priority=u)
        return carry

    lax.fori_loop(0, n // 2, body, 0, unroll=4)


def _start_gather(pos_ref, src_hbm, dst_vmem, sem):
    def make_copy(r):
        return pltpu.make_async_copy(_token_tile(src_hbm, pos_ref[0, 0, r]), _token_tile(dst_vmem, r), sem)

    _start_row_copies(dst_vmem.shape[0] // SUBLANES, make_copy)


def _wait_gather(src_hbm, dst_vmem, sem):
    pltpu.make_async_copy(src_hbm.at[pl.ds(0, dst_vmem.shape[0])], dst_vmem, sem).wait()


def _pos_specs(n_steps, tm):
    return [pl.BlockSpec((1, 1, tm), lambda i: (i, 0, 0), memory_space=pltpu.SMEM),
            pl.BlockSpec((1, 1, tm), lambda i: (jnp.minimum(i + 1, n_steps - 1), 0, 0),
                         memory_space=pltpu.SMEM)]


def _gathered_tile(pos_ref, pos_next_ref, src_hbm, buf, sem):
    i = pl.program_id(0)
    slot = i & 1
    pl.when(i == 0)(lambda: _start_gather(pos_ref, src_hbm, buf.at[0], sem.at[0]))

    @pl.when(i + 1 < pl.num_programs(0))
    def _():
        _start_gather(pos_next_ref, src_hbm, buf.at[1 - slot], sem.at[1 - slot])

    _wait_gather(src_hbm, buf.at[slot], sem.at[slot])
    return _from_token_tiles(buf.at[slot])


def _mem_attn(qm, kv):
    outs = []
    for hh in range(MEM_HEADS):
        lo, hi = hh * MEM_HEAD_DIM, (hh + 1) * MEM_HEAD_DIM
        q = (qm[:, lo:hi] * (MEM_HEAD_DIM ** -0.5)).astype(BF16)
        k = kv[:, lo:hi]
        v = kv[:, MEM_WIDTH + lo:MEM_WIDTH + hi]
        s = _dot_nt(q, k)
        m = jnp.max(s, axis=-1, keepdims=True)
        p = jnp.exp(s - m)
        l = jnp.sum(p, axis=-1, keepdims=True)
        outs.append(_dot(p.astype(BF16), v) / l)
    return jnp.concatenate(outs, axis=-1)


def _cast_weight_once(w_ref, wb_sc):
    @pl.when(pl.program_id(0) == 0)
    def _():
        wb_sc[...] = w_ref[0].astype(BF16)


def _project(h, g_ref, wb_sc, kv_ref, out_refs, splits, scales):
    hn = _rms(h, g_ref[...]).astype(BF16)
    proj = _dot(hn, wb_sc[...])
    off = 0
    for ref, width, scale in zip(out_refs[:-1], splits, scales):
        piece = proj[:, off:off + width]
        if scale != 1.0:
            piece = piece * scale
        ref[...] = piece.astype(ref.dtype)
        off += width
    out_refs[-1][...] = _mem_attn(proj[:, off:off + MEM_WIDTH], kv_ref[0, 0]).astype(BF16)


def _in_kernel(h_ref, g_ref, w_ref, kv_ref, *rest, splits, scales):
    out_refs, wb_sc = rest[:-1], rest[-1]
    _cast_weight_once(w_ref, wb_sc)
    _project(h_ref[...], g_ref, wb_sc, kv_ref, out_refs, splits, scales)


def _in_moe_kernel(pos_ref, pos_next_ref, h2_ref, ys_ref, g_ref, w_ref, kv_ref, h_ref, *rest, splits, scales):
    out_refs, (ybuf, sem, wb_sc) = rest[:-3], rest[-3:]
    _cast_weight_once(w_ref, wb_sc)
    h = h2_ref[...] + _gathered_tile(pos_ref, pos_next_ref, ys_ref, ybuf, sem)
    h_ref[...] = h
    _project(h, g_ref, wb_sc, kv_ref, out_refs, splits, scales)


def _in_proj(h, moe, g, w_all, w_idx, kv, layer, batch, splits, scales, dtypes, tm):
    t, d = h.shape
    n = w_all.shape[2]
    tiles_per_batch = (t // batch) // tm
    mem_len = kv.shape[2]
    row = lambda i: (i, 0)
    const = lambda i: (0, 0)
    out_shape = [jax.ShapeDtypeStruct((t, w), dt) for w, dt in zip(splits, dtypes)]
    out_shape.append(jax.ShapeDtypeStruct((t, MEM_WIDTH), BF16))
    out_specs = [pl.BlockSpec((tm, w), row) for w in splits]
    out_specs.append(pl.BlockSpec((tm, MEM_WIDTH), row))
    in_specs = [pl.BlockSpec((tm, d), row),
                pl.BlockSpec((1, d), const),
                pl.BlockSpec((1, d, n), lambda i: (w_idx, 0, 0), pipeline_mode=pl.Buffered(1)),
                pl.BlockSpec((1, 1, mem_len, 2 * MEM_WIDTH),
                             lambda i: (layer, i // tiles_per_batch, 0, 0))]
    if moe is None:
        return pl.pallas_call(
            functools.partial(_in_kernel, splits=splits, scales=scales),
            grid=(t // tm,), in_specs=in_specs, out_specs=out_specs, out_shape=out_shape,
            scratch_shapes=[pltpu.VMEM((d, n), BF16)],
            compiler_params=pltpu.CompilerParams(dimension_semantics=("arbitrary",)),
            name="in_proj",
        )(h, g, w_all, kv)
    ys, pos = moe
    pos3 = pos.reshape(t // tm, 1, tm)
    in_specs = _pos_specs(t // tm, tm) + [in_specs[0], pl.BlockSpec(memory_space=pl.ANY)] + in_specs[1:]
    return pl.pallas_call(
        functools.partial(_in_moe_kernel, splits=splits, scales=scales),
        grid=(t // tm,), in_specs=in_specs,
        out_specs=[pl.BlockSpec((tm, d), row)] + out_specs,
        out_shape=[jax.ShapeDtypeStruct((t, d), F32)] + out_shape,
        scratch_shapes=[pltpu.VMEM((2, tm * SUBLANES, LANES), F32), pltpu.SemaphoreType.DMA((2,)),
                        pltpu.VMEM((d, n), BF16)],
        compiler_params=pltpu.CompilerParams(dimension_semantics=("arbitrary",)),
        name="in_proj_moe_gather",
    )(pos3, pos3, h, ys, g, w_all, kv)


def _gelu_tanh(x):
    return 0.5 * x * (1.0 + jnp.tanh(math.sqrt(2.0 / math.pi) * (x + 0.044715 * (x * x * x))))


def _softplus(z):
    return jnp.maximum(z, 0.0) + jnp.log1p(jnp.exp(-jnp.abs(z)))


def _rglru_kernel(ux_ref, ug_ref, cw_ref, cb_ref, wa_ref, ba_ref, wx_ref, bx_ref, lam_ref,
                  o_ref, xpad_sc, h_sc):
    ts, c = ux_ref.shape
    pad = 8

    @pl.when(pl.program_id(1) == 0)
    def _():
        xpad_sc[0:pad, :] = jnp.zeros((pad, c), F32)
        h_sc[...] = jnp.zeros_like(h_sc)

    ux = ux_ref[...]
    xpad_sc[pad:pad + ts, :] = ux
    xc = cb_ref[...] + cw_ref[CONV_WIDTH - 1:CONV_WIDTH, :] * ux
    for j in range(CONV_WIDTH - 1):
        xc = xc + cw_ref[j:j + 1, :] * xpad_sc[pl.ds(pad - (CONV_WIDTH - 1) + j, ts), :]
    xpad_sc[0:pad, :] = ux[ts - pad:ts, :]

    xcb = xc.astype(BF16)
    r = jax.nn.sigmoid(_dot(xcb, wa_ref[...]) + ba_ref[...])
    ig = jax.nn.sigmoid(_dot(xcb, wx_ref[...]) + bx_ref[...])
    log_a = (-LRU_C) * r * _softplus(-lam_ref[...])
    a = jnp.exp(log_a)
    b = jnp.sqrt(-jnp.tanh(log_a) * (a * a + 1.0)) * (ig * xc)

    row = lax.broadcasted_iota(jnp.int32, (ts, c), 0) & (SUBLANES - 1)
    k = 1
    while k < SUBLANES:
        keep = row >= k
        a_s = jnp.where(keep, pltpu.roll(a, k, 0), 1.0)
        b_s = jnp.where(keep, pltpu.roll(b, k, 0), 0.0)
        b = a * b_s + b
        a = a * a_s
        k *= 2
    h_in = h_sc[...]
    groups = []
    for g in range(ts // SUBLANES):
        rows = slice(g * SUBLANES, (g + 1) * SUBLANES)
        groups.append(b[rows] + a[rows] * h_in)
        h_in = groups[-1][SUBLANES - 1:SUBLANES, :]
    h_sc[...] = h_in
    o_ref[...] = (jnp.concatenate(groups, axis=0) * _gelu_tanh(ug_ref[...])).astype(o_ref.dtype)


def _rglru(ux, ug, cw, cb, wa, ba, wx, bx, lam, batch, ts):
    t, c = ux.shape
    nt = (t // batch) // ts
    row = lambda b, s: (b * nt + s, 0)
    const = lambda b, s: (0, 0)
    return pl.pallas_call(
        _rglru_kernel,
        grid=(batch, nt),
        in_specs=[pl.BlockSpec((ts, c), row), pl.BlockSpec((ts, c), row),
                  pl.BlockSpec((CONV_WIDTH, c), const), pl.BlockSpec((1, c), const),
                  pl.BlockSpec((c, c), const), pl.BlockSpec((1, c), const),
                  pl.BlockSpec((c, c), const), pl.BlockSpec((1, c), const),
                  pl.BlockSpec((1, c), const)],
        out_specs=pl.BlockSpec((ts, c), row),
        out_shape=jax.ShapeDtypeStruct((t, c), BF16),
        scratch_shapes=[pltpu.VMEM((ts + 8, c), F32), pltpu.VMEM((1, c), F32)],
        compiler_params=pltpu.CompilerParams(dimension_semantics=("arbitrary", "arbitrary")),
        name="rglru",
    )(ux, ug, cw, cb, wa, ba, wx, bx, lam)


def _block_diag(w):
    g, n, _ = w.shape
    eye = jnp.eye(g, dtype=w.dtype)
    return (eye[:, None, :, None] * w[:, :, None, :]).reshape(g * n, g * n)


def _bucket_table():
    i = np.arange(REL_MAX_DIST)[:, None]
    j = np.arange(REL_MAX_DIST)[None, :]

    def bucket(n):
        nf = np.maximum(n, 1).astype(np.float32)
        large = REL_MAX_EXACT + (np.log(nf / REL_MAX_EXACT) / math.log(REL_MAX_DIST / REL_MAX_EXACT)
                                 * (REL_BUCKETS - REL_MAX_EXACT)).astype(np.int32)
        return np.where(n < REL_MAX_EXACT, n, np.minimum(large, REL_BUCKETS - 1))

    diag = np.where(j <= i, bucket(np.maximum(i - j, 0)), -1)
    below = bucket(REL_MAX_DIST + i - j)
    return np.stack([diag, below]).astype(np.int32)


def _bias_kernel(rb_ref, bucket_ref, o_ref):
    h = pl.program_id(0)
    bk = bucket_ref[...]
    far = rb_ref[REL_BUCKETS - 1, h]
    acc = jnp.zeros(bk.shape, F32)
    for b in range(REL_BUCKETS - 1):
        acc = jnp.where(bk == b, rb_ref[b, h] - far, acc)
    blocks = jnp.where(bk < 0, NEG, acc * LOG2E)
    blk = REL_MAX_DIST
    tile = o_ref.shape[-1]
    o_ref[...] = jnp.zeros_like(o_ref)
    for a in range(tile // blk):
        rows = slice(a * blk, (a + 1) * blk)
        o_ref[0, 0, rows, rows] = blocks[0]
        if a >= 1:
            o_ref[0, 0, rows, (a - 1) * blk:a * blk] = blocks[1]
        if (a + 1) * blk < tile:
            o_ref[0, 0, rows, (a + 1) * blk:] = jnp.full((blk, tile - (a + 1) * blk), NEG, F32)
    o_ref[0, 1, 0:blk, tile - blk:] = blocks[1]


def _bias_tiles(rel_bias, tile):
    assert tile % REL_MAX_DIST == 0
    heads = rel_bias.shape[1]
    buckets = jnp.asarray(_bucket_table())
    return pl.pallas_call(
        _bias_kernel,
        grid=(heads,),
        in_specs=[pl.BlockSpec(memory_space=pltpu.SMEM),
                  pl.BlockSpec(buckets.shape, lambda h: (0, 0, 0))],
        out_specs=pl.BlockSpec((1, 2, tile, tile), lambda h: (h, 0, 0, 0)),
        out_shape=jax.ShapeDtypeStruct((heads, 2, tile, tile), F32),
        name="rel_bias_tiles",
    )(rel_bias, buckets)


def _attn_kernel(q_ref, k_ref, v_ref, bias_ref, lq1_ref, lk1_ref, lq2_ref, lk2_ref,
                 g_ref, o_ref, m_sc, acc_sc, sa_sc, sb_sc, pma_sc, pmb_sc, *, lambda_init):
    qi = pl.program_id(2)
    tile = q_ref.shape[1]
    reps = tile // LANES
    m_sc[...] = jnp.full(m_sc.shape, NEG, F32)
    acc_sc[...] = jnp.zeros_like(acc_sc)

    def lane_tile_max(s):
        parts = [s[:, c * LANES:(c + 1) * LANES] for c in range(s.shape[1] // LANES)]
        while len(parts) > 1:
            parts = [jnp.maximum(a, b) for a, b in zip(parts[::2], parts[1::2])]
        return parts[0]

    def scores(ki, bufs, bias_slot=None):
        s_ref, pm_ref = bufs
        start = pl.multiple_of(ki * tile, tile)
        q = q_ref[0]
        k = k_ref[0, pl.ds(start, tile), :]
        for j in range(2):
            lo, hi = j * DIFF_QK_DIM, (j + 1) * DIFF_QK_DIM
            s = _dot_nt(q[:, lo:hi], k[:, lo:hi])
            if bias_slot is not None:
                s = s + bias_ref[0, bias_slot]
            s_ref[j] = s
            pm_ref[j] = lane_tile_max(s)

    def accumulate(ki, bufs, bias_slot=None):
        s_ref, pm_ref = bufs
        start = pl.multiple_of(ki * tile, tile)
        v = v_ref[0, pl.ds(start, tile), :]
        v1 = jnp.concatenate([v, jnp.ones_like(v)], axis=-1)
        for j in range(2):
            s = s_ref[j]
            if bias_slot is not None:
                s = s + bias_ref[0, bias_slot]
                pm = lane_tile_max(s)
            else:
                pm = pm_ref[j]
            m_prev = m_sc[j]
            m_new = jnp.maximum(m_prev, jnp.max(pm, axis=-1, keepdims=True))
            alpha = jnp.exp2(m_prev - m_new)
            p = jnp.exp2(s - jnp.concatenate([m_new] * reps, axis=1))
            acc_sc[j] = jnp.concatenate([alpha, alpha], axis=1) * acc_sc[j] + _dot(p.astype(BF16), v1)
            m_sc[j] = m_new

    buf_a, buf_b = (sa_sc, pma_sc), (sb_sc, pmb_sc)
    n_far = jnp.maximum(qi - 1, 0)
    odd = n_far & 1
    pl.when(odd == 0)(lambda: scores(0, buf_a))

    @pl.when(odd == 1)
    def _():
        scores(0, buf_b)
        scores(1, buf_a)
        accumulate(0, buf_b)

    def far_pair(p, carry):
        base = odd + 2 * p
        scores(base + 1, buf_b)
        accumulate(base, buf_a)
        scores(base + 2, buf_a)
        accumulate(base + 1, buf_b)
        return carry

    lax.fori_loop(0, n_far >> 1, far_pair, 0)

    @pl.when(qi >= 1)
    def _():
        scores(qi, buf_b, bias_slot=0)
        accumulate(qi - 1, buf_a, bias_slot=1)
        accumulate(qi, buf_b)

    @pl.when(qi == 0)
    def _():
        accumulate(0, buf_a, bias_slot=0)

    lam = (jnp.exp(jnp.sum(lq1_ref[...] * lk1_ref[...], axis=-1, keepdims=True))
           - jnp.exp(jnp.sum(lq2_ref[...] * lk2_ref[...], axis=-1, keepdims=True))
           + lambda_init)
    dv = o_ref.shape[-1]
    o = (acc_sc[0, :, :dv] / acc_sc[0, :, dv:]) - lam * (acc_sc[1, :, :dv] / acc_sc[1, :, dv:])
    o_ref[0] = (_rms(o, g_ref[...]) * (1.0 - lambda_init)).astype(o_ref.dtype)


def _diff_attention(q, k, v, bias, lq1, lk1, lq2, lk2, g, lambda_init, tile):
    batch, s, width = q.shape
    heads = width // DIFF_V_DIM
    vec = lambda b, h, i: (0, 0)
    head_all = lambda b, h, i: (b, 0, h)
    return pl.pallas_call(
        functools.partial(_attn_kernel, lambda_init=lambda_init),
        grid=(batch, heads, s // tile),
        in_specs=[pl.BlockSpec((1, tile, DIFF_V_DIM), lambda b, h, i: (b, i, h)),
                  pl.BlockSpec((1, s, DIFF_V_DIM), head_all),
                  pl.BlockSpec((1, s, DIFF_V_DIM), head_all),
                  pl.BlockSpec((1, 2, tile, tile), lambda b, h, i: (h, 0, 0, 0)),
                  pl.BlockSpec((1, DIFF_QK_DIM), vec), pl.BlockSpec((1, DIFF_QK_DIM), vec),
                  pl.BlockSpec((1, DIFF_QK_DIM), vec), pl.BlockSpec((1, DIFF_QK_DIM), vec),
                  pl.BlockSpec((1, DIFF_V_DIM), vec)],
        out_specs=pl.BlockSpec((1, tile, DIFF_V_DIM), lambda b, h, i: (b, i, h)),
        out_shape=jax.ShapeDtypeStruct((batch, s, width), BF16),
        scratch_shapes=[pltpu.VMEM((2, tile, LANES), F32),
                        pltpu.VMEM((2, tile, 2 * DIFF_V_DIM), F32),
                        pltpu.VMEM((2, tile, tile), F32), pltpu.VMEM((2, tile, tile), F32),
                        pltpu.VMEM((2, tile, LANES), F32), pltpu.VMEM((2, tile, LANES), F32)],
        compiler_params=pltpu.CompilerParams(
            dimension_semantics=("arbitrary", "arbitrary", "arbitrary")),
        name="diff_attn",
    )(q, k, v, bias, lq1, lk1, lq2, lk2, g)


def _router_gates(logits, n_experts):
    lane = lax.broadcasted_iota(jnp.int32, logits.shape, 1)
    lane_f = lane.astype(F32)
    is_g = (lane >= n_experts) & (lane < n_experts + N_GROUPS)
    gl = jnp.where(is_g, logits, NEG)
    gmax = jnp.max(gl, axis=-1, keepdims=True)
    gidx = jnp.min(jnp.where(gl == gmax, lane_f - n_experts, 1e9), axis=-1, keepdims=True)
    gsum = jnp.sum(jnp.where(is_g, jnp.exp(gl - gmax), 0.0), axis=-1, keepdims=True)
    g_w = 1.0 / gsum
    in_grp = (lane < n_experts) & ((lane // EXPERTS_PER_GROUP).astype(F32) == gidx)
    el = jnp.where(in_grp, logits, NEG)
    t1 = jnp.max(el, axis=-1, keepdims=True)
    i1 = jnp.min(jnp.where(el == t1, lane_f, 1e9), axis=-1, keepdims=True)
    el2 = jnp.where(lane_f == i1, NEG, el)
    t2 = jnp.max(el2, axis=-1, keepdims=True)
    i2 = jnp.min(jnp.where(el2 == t2, lane_f, 1e9), axis=-1, keepdims=True)
    e2 = jnp.exp(t2 - t1)
    w1 = g_w / (1.0 + e2)
    w2 = g_w * e2 / (1.0 + e2)
    return jnp.where(lane_f == i1, w1, jnp.where(lane_f == i2, w2, 0.0)), gidx


def _out_kernel(tok_ref, mo_ref, h_ref, wo_ref, g_ref, wr_ref, h2_ref, hn_ref, route_ref, cnt_sc, wob_sc, *,
                n_experts):
    @pl.when(pl.program_id(0) == 0)
    def _():
        cnt_sc[...] = jnp.zeros_like(cnt_sc)

    _cast_weight_once(wo_ref, wob_sc)
    tw = tok_ref.shape[1]
    y = _dot(tok_ref[...], wob_sc[0:tw, :]) + _dot(mo_ref[...], wob_sc[tw:, :])
    h2 = h_ref[...] + y
    h2_ref[...] = h2
    hn = _rms(h2, g_ref[...])
    _to_token_tiles(hn_ref, hn)
    _, gidx = _router_gates(_dot(hn.astype(BF16), wr_ref[...]), n_experts)

    tm = h2.shape[0]
    lane = lax.broadcasted_iota(jnp.int32, (tm, LANES), 1)
    onehot = jnp.where(lane.astype(F32) == gidx, 1.0, 0.0)
    earlier = lax.broadcasted_iota(jnp.int32, (tm, tm), 1) < lax.broadcasted_iota(jnp.int32, (tm, tm), 0)
    before = _dot(jnp.where(earlier, 1.0, 0.0).astype(BF16), onehot.astype(BF16)) + cnt_sc[...]
    rank = jnp.sum(onehot * before, axis=-1, keepdims=True)
    cnt_sc[...] += jnp.sum(onehot, axis=0, keepdims=True)
    route_ref[...] = jnp.where(lane == 0, gidx, jnp.where(lane == 1, rank, 0.0))


def _out_proj(tok, mo, h, wo_all, layer, g, wr_bf16, n_experts, tm):
    t, d = h.shape
    tw, mw = tok.shape[1], mo.shape[1]
    row = lambda i: (i, 0)
    const = lambda i: (0, 0)
    return pl.pallas_call(
        functools.partial(_out_kernel, n_experts=n_experts),
        grid=(t // tm,),
        in_specs=[pl.BlockSpec((tm, tw), row), pl.BlockSpec((tm, mw), row), pl.BlockSpec((tm, d), row),
                  pl.BlockSpec((1, d, d), lambda i: (layer, 0, 0), pipeline_mode=pl.Buffered(1)),
                  pl.BlockSpec((1, d), const), pl.BlockSpec((d, LANES), const)],
        out_specs=[pl.BlockSpec((tm, d), row), pl.BlockSpec((tm * SUBLANES, LANES), row),
                   pl.BlockSpec((tm, LANES), row)],
        out_shape=[jax.ShapeDtypeStruct((t, d), F32), jax.ShapeDtypeStruct((t * SUBLANES, LANES), F32),
                   jax.ShapeDtypeStruct((t, LANES), F32)],
        scratch_shapes=[pltpu.VMEM((1, LANES), F32), pltpu.VMEM((d, d), BF16)],
        compiler_params=pltpu.CompilerParams(dimension_semantics=("arbitrary",)),
        name="out_proj_router",
    )(tok, mo, h, wo_all, g, wr_bf16)


def _route_tables(route, tile, n_tiles):
    gidx = route[:, 0].astype(jnp.int32)
    rank = route[:, 1].astype(jnp.int32)
    onehot = gidx[:, None] == jnp.arange(N_GROUPS, dtype=jnp.int32)[None, :]
    cnt = jnp.sum(onehot, axis=0, dtype=jnp.int32)
    ptiles = (cnt + tile - 1) // tile
    tile_end = jnp.cumsum(ptiles)
    off = (tile_end - ptiles) * tile
    pos = jnp.sum(jnp.where(onehot, off[None, :], 0), axis=1) + rank
    tiles = jnp.arange(n_tiles, dtype=jnp.int32)
    tile_group = jnp.minimum(jnp.sum(tiles[:, None] >= tile_end[None, :], axis=1), N_GROUPS - 1)
    return pos.astype(jnp.int32), tile_group.astype(jnp.int32), tile_end[-1:].astype(jnp.int32)


def _sort_kernel(pos_ref, x_ref, xs_in_ref, xs_ref, sem):
    del xs_in_ref

    def make_copy(r):
        return pltpu.make_async_copy(_token_tile(x_ref, r), _token_tile(xs_ref, pos_ref[0, 0, r]), sem)

    n = x_ref.shape[0]
    _start_row_copies(n // SUBLANES, make_copy)
    pltpu.make_async_copy(x_ref, xs_ref.at[pl.ds(0, n)], sem).wait()


def _sort_rows(x, pos, n_tokens_out, tm):
    t = pos.shape[0]
    return pl.pallas_call(
        _sort_kernel,
        grid=(t // tm,),
        in_specs=[pl.BlockSpec((1, 1, tm), lambda i: (i, 0, 0), memory_space=pltpu.SMEM),
                  pl.BlockSpec((tm * SUBLANES, LANES), lambda i: (i, 0)),
                  pl.BlockSpec(memory_space=pl.ANY)],
        out_specs=pl.BlockSpec(memory_space=pl.ANY),
        out_shape=jax.ShapeDtypeStruct((n_tokens_out * SUBLANES, LANES), x.dtype),
        input_output_aliases={2: 0},
        scratch_shapes=[pltpu.SemaphoreType.DMA],
        compiler_params=pltpu.CompilerParams(dimension_semantics=("arbitrary",)),
        name="moe_sort_rows",
    )(pos.reshape(t // tm, 1, tm), x, jnp.zeros((n_tokens_out * SUBLANES, LANES), x.dtype))


def _experts_kernel(tg_ref, na_ref, xs_ref, wr_ref, w1_ref, w3_ref, w2_ref, ys_ref, w13b_sc, w2b_sc, *,
                    n_experts):
    i = pl.program_id(0)
    f = w2_ref.shape[2]

    @pl.when((i == 0) | (tg_ref[i] != tg_ref[jnp.maximum(i - 1, 0)]))
    def _():
        for j in range(EXPERTS_PER_GROUP):
            w13b_sc[j, :, 0:f] = w1_ref[0, j].astype(BF16)
            w13b_sc[j, :, f:2 * f] = w3_ref[0, j].astype(BF16)
            w2b_sc[j * f:(j + 1) * f, :] = w2_ref[0, j].astype(BF16)

    @pl.when(i >= na_ref[0])
    def _():
        ys_ref[...] = jnp.zeros_like(ys_ref)

    @pl.when(i < na_ref[0])
    def _():
        x = _from_token_tiles(xs_ref).astype(BF16)
        gates, _ = _router_gates(_dot(x, wr_ref[...]), n_experts)
        lane = lax.broadcasted_iota(jnp.int32, gates.shape, 1)
        first = tg_ref[i] * EXPERTS_PER_GROUP
        hid = []
        for j in range(EXPERTS_PER_GROUP):
            au = _dot(x, w13b_sc[j])
            a = au[:, :f]
            u = au[:, f:]
            gate = jnp.sum(jnp.where(lane == first + j, gates, 0.0), axis=-1, keepdims=True)
            hid.append(((a * jax.nn.sigmoid(a)) * u * gate).astype(BF16))
        _to_token_tiles(ys_ref, _dot(jnp.concatenate(hid, axis=-1), w2b_sc[...]))


def _experts(xs, tile_group, n_active, wr, w1_all, w3_all, w2_all, layer, n_experts, tile):
    f, d = w2_all.shape[2:]
    rows = xs.shape[0] // SUBLANES
    epg = EXPERTS_PER_GROUP
    group_w = lambda i, tg, na: (layer, tg[i], 0, 0)
    grid_spec = pltpu.PrefetchScalarGridSpec(
        num_scalar_prefetch=2,
        grid=(rows // tile,),
        in_specs=[pl.BlockSpec((tile * SUBLANES, LANES), lambda i, tg, na: (i, 0)),
                  pl.BlockSpec((d, LANES), lambda i, tg, na: (0, 0)),
                  pl.BlockSpec((1, epg, d, f), group_w),
                  pl.BlockSpec((1, epg, d, f), group_w),
                  pl.BlockSpec((1, epg, f, d), group_w)],
        out_specs=pl.BlockSpec((tile * SUBLANES, LANES), lambda i, tg, na: (i, 0)),
        scratch_shapes=[pltpu.VMEM((epg, d, 2 * f), BF16), pltpu.VMEM((epg * f, d), BF16)],
    )
    return pl.pallas_call(
        functools.partial(_experts_kernel, n_experts=n_experts),
        grid_spec=grid_spec,
        out_shape=jax.ShapeDtypeStruct(xs.shape, F32),
        compiler_params=pltpu.CompilerParams(dimension_semantics=("arbitrary",)),
        name="moe_experts",
    )(tile_group, n_active, xs, wr, w1_all, w3_all, w2_all)


def _final_kernel(pos_ref, pos_next_ref, h2_ref, ys_ref, g_ref, o_ref, ybuf, sem):
    o_ref[...] = _rms(h2_ref[...] + _gathered_tile(pos_ref, pos_next_ref, ys_ref, ybuf, sem), g_ref[...])


def _final_norm(h2, ys, pos, g, tm):
    t, d = h2.shape
    pos3 = pos.reshape(t // tm, 1, tm)
    return pl.pallas_call(
        _final_kernel,
        grid=(t // tm,),
        in_specs=_pos_specs(t // tm, tm) + [pl.BlockSpec((tm, d), lambda i: (i, 0)),
                                            pl.BlockSpec(memory_space=pl.ANY),
                                            pl.BlockSpec((1, d), lambda i: (0, 0))],
        out_specs=pl.BlockSpec((tm, d), lambda i: (i, 0)),
        out_shape=jax.ShapeDtypeStruct((t, d), F32),
        scratch_shapes=[pltpu.VMEM((2, tm * SUBLANES, LANES), F32), pltpu.SemaphoreType.DMA((2,))],
        compiler_params=pltpu.CompilerParams(dimension_semantics=("arbitrary",)),
        name="final_norm_moe_gather",
    )(pos3, pos3, h2, ys, g)


def kernel(x, mem, norm_mix_g, w_in_a, w_in_b, w_out, conv_w, conv_b, gate_a_w, gate_a_b, gate_x_w, gate_x_b, lru_lambda, diff_lq1, diff_lk1, diff_lq2, diff_lk2, diff_subln_g, rel_bias, mem_norm_g, w_mem_kv, norm_ffn_g, w_group, w_router, w_exp1, w_exp3, w_exp2, final_norm_g):
    batch, seq, d = x.shape
    depth = w_out.shape[0]
    n_experts = w_router.shape[-1]
    t = batch * seq
    rnn_width = conv_w.shape[-1]
    tok_width = d - MEM_WIDTH
    tm = min(512, seq)
    ts = min(256, seq)
    attn_tile = min(1024, seq)
    moe_tile = min(512, seq)
    moe_tiles = t // moe_tile + N_GROUPS

    kv = _memkv(mem.reshape(-1, d), mem_norm_g.reshape(1, d), w_mem_kv)
    kv = kv.reshape(depth, batch, mem.shape[1], 2 * MEM_WIDTH)
    bias = _bias_tiles(rel_bias, attn_tile)

    h = x.reshape(t, d)
    moe = None
    for i in range(depth):
        j = i // 2
        g_mix = norm_mix_g[i].reshape(1, d)
        if i % 2 == 0:
            outs = _in_proj(h, moe, g_mix, w_in_a, j, kv, i, batch,
                            (rnn_width, rnn_width), (1.0, 1.0), (F32, F32), tm)
            if moe is not None:
                h, outs = outs[0], outs[1:]
            ux, ug, mo = outs
            tok = _rglru(ux, ug, conv_w[j], conv_b[j].reshape(1, -1),
                         _block_diag(gate_a_w[j]).astype(BF16), gate_a_b[j].reshape(1, -1),
                         _block_diag(gate_x_w[j]).astype(BF16), gate_x_b[j].reshape(1, -1),
                         lru_lambda[j].reshape(1, -1), batch, ts)
        else:
            outs = _in_proj(h, moe, g_mix, w_in_b, j, kv, i, batch,
                            (tok_width, tok_width, tok_width),
                            (DIFF_QK_DIM ** -0.5 * LOG2E, 1.0, 1.0), (BF16, BF16, BF16), tm)
            if moe is not None:
                h, outs = outs[0], outs[1:]
            q, k, v, mo = outs
            lambda_init = 0.8 - 0.6 * math.exp(-0.3 * i)
            shp = (batch, seq, tok_width)
            tok = _diff_attention(q.reshape(shp), k.reshape(shp), v.reshape(shp), bias,
                                  diff_lq1[j].reshape(1, -1), diff_lk1[j].reshape(1, -1),
                                  diff_lq2[j].reshape(1, -1), diff_lk2[j].reshape(1, -1),
                                  diff_subln_g[j].reshape(1, -1), lambda_init, attn_tile)
            tok = tok.reshape(t, tok_width)
        wr = jnp.concatenate([w_router[i], w_group[i]], axis=-1)
        wr = jnp.pad(wr, ((0, 0), (0, LANES - wr.shape[1]))).astype(BF16)
        h, hn, route = _out_proj(tok, mo, h, w_out, i, norm_ffn_g[i].reshape(1, d), wr, n_experts, tm)
        pos, tile_group, n_active = _route_tables(route, moe_tile, moe_tiles)
        assert d == SUBLANES * LANES
        xs = _sort_rows(hn, pos, moe_tiles * moe_tile, tm)
        ys = _experts(xs, tile_group, n_active, wr, w_exp1, w_exp3, w_exp2, i, n_experts, moe_tile)
        moe = (ys, pos)
    return _final_norm(h, moe[0], moe[1], final_norm_g.reshape(1, d), tm).reshape(batch, seq, d)
```

```python
import functools
import math

import numpy as np
import jax
import jax.numpy as jnp
from jax import lax
from jax.experimental import pallas as pl
from jax.experimental.pallas import tpu as pltpu

F32 = jnp.float32
BF16 = jnp.bfloat16
EPS = 1e-6
NEG = -1e30
LOG2E = math.log2(math.e)

MEM_HEADS = 4
MEM_HEAD_DIM = 64
MEM_WIDTH = MEM_HEADS * MEM_HEAD_DIM
CONV_WIDTH = 4
LRU_C = 8.0
DIFF_QK_DIM = 64
DIFF_V_DIM = 2 * DIFF_QK_DIM
REL_BUCKETS = 32
REL_MAX_EXACT = 16
REL_MAX_DIST = 128
N_GROUPS = 4
EXPERTS_PER_GROUP = 4
LANES = 128
SUBLANES = 8
ROUTE_WIDTH = 8


def _rms(x, g):
    return x * lax.rsqrt(jnp.mean(x * x, axis=-1, keepdims=True) + EPS) * g


def _dot(a, b):
    return jnp.dot(a, b, preferred_element_type=F32)


def _dot_nt(a, b):
    return lax.dot_general(a, b, (((1,), (1,)), ((), ())), preferred_element_type=F32)


def _memkv_kernel(mem_ref, g_ref, w_ref, o_ref):
    xn = _rms(mem_ref[...], g_ref[...]).astype(BF16)
    o_ref[0] = _dot(xn, w_ref[0].astype(BF16)).astype(BF16)


def _memkv(mem2d, g, w):
    depth, d, n = w.shape
    rows = mem2d.shape[0]
    return pl.pallas_call(
        _memkv_kernel,
        grid=(depth,),
        in_specs=[pl.BlockSpec((rows, d), lambda i: (0, 0)),
                  pl.BlockSpec((1, d), lambda i: (0, 0)),
                  pl.BlockSpec((1, d, n), lambda i: (i, 0, 0))],
        out_specs=pl.BlockSpec((1, rows, n), lambda i: (i, 0, 0)),
        out_shape=jax.ShapeDtypeStruct((depth, rows, n), BF16),
        name="memkv",
    )(mem2d, g, w)


def _to_token_tiles(ref, x):
    n = x.shape[0]
    for c in range(SUBLANES):
        ref[pl.ds(c, n, stride=SUBLANES), :] = x[:, c * LANES:(c + 1) * LANES]


def _from_token_tiles(ref):
    n = ref.shape[0] // SUBLANES
    return jnp.concatenate([ref[pl.ds(c, n, stride=SUBLANES), :] for c in range(SUBLANES)], axis=1)


def _token_tile(ref, t):
    return ref.at[pl.ds(pl.multiple_of(t * SUBLANES, SUBLANES), SUBLANES)]


def _start_row_copies(n, make_copy):
    def body(p, carry):
        for u in range(2):
            make_copy(2 * p + u).start(priority=u)
        return carry

    lax.fori_loop(0, n // 2, body, 0, unroll=4)


def _start_gather(pos_ref, src_hbm, dst_vmem, sem):
    def make_copy(r):
        return pltpu.make_async_copy(_token_tile(src_hbm, pos_ref[0, 0, r]), _token_tile(dst_vmem, r), sem)

    _start_row_copies(dst_vmem.shape[0] // SUBLANES, make_copy)


def _wait_gather(src_hbm, dst_vmem, sem):
    pltpu.make_async_copy(src_hbm.at[pl.ds(0, dst_vmem.shape[0])], dst_vmem, sem).wait()


def _pos_specs(n_steps, tm):
    return [pl.BlockSpec((1, 1, tm), lambda i: (i, 0, 0), memory_space=pltpu.SMEM),
            pl.BlockSpec((1, 1, tm), lambda i: (jnp.minimum(i + 1, n_steps - 1), 0, 0),
                         memory_space=pltpu.SMEM)]


def _gathered_tile(pos_ref, pos_next_ref, src_hbm, buf, sem):
    i = pl.program_id(0)
    slot = i & 1
    pl.when(i == 0)(lambda: _start_gather(pos_ref, src_hbm, buf.at[0], sem.at[0]))

    @pl.when(i + 1 < pl.num_programs(0))
    def _():
        _start_gather(pos_next_ref, src_hbm, buf.at[1 - slot], sem.at[1 - slot])

    _wait_gather(src_hbm, buf.at[slot], sem.at[slot])
    return _from_token_tiles(buf.at[slot])


def _mem_attn(qm, kv):
    outs = []
    for hh in range(MEM_HEADS):
        lo, hi = hh * MEM_HEAD_DIM, (hh + 1) * MEM_HEAD_DIM
        q = (qm[:, lo:hi] * (MEM_HEAD_DIM ** -0.5)).astype(BF16)
        k = kv[:, lo:hi]
        v = kv[:, MEM_WIDTH + lo:MEM_WIDTH + hi]
        s = _dot_nt(q, k)
        m = jnp.max(s, axis=-1, keepdims=True)
        p = jnp.exp(s - m)
        l = jnp.sum(p, axis=-1, keepdims=True)
        outs.append(_dot(p.astype(BF16), v) / l)
    return jnp.concatenate(outs, axis=-1)


def _cast_weight_once(w_ref, wb_sc):
    @pl.when(pl.program_id(0) == 0)
    def _():
        wb_sc[...] = w_ref[0].astype(BF16)


def _project(h, g_ref, wb_sc, kv_ref, out_refs, splits, scales):
    hn = _rms(h, g_ref[...]).astype(BF16)
    proj = _dot(hn, wb_sc[...])
    off = 0
    for ref, width, scale in zip(out_refs[:-1], splits, scales):
        piece = proj[:, off:off + width]
        if scale != 1.0:
            piece = piece * scale
        ref[...] = piece.astype(ref.dtype)
        off += width
    out_refs[-1][...] = _mem_attn(proj[:, off:off + MEM_WIDTH], kv_ref[0, 0]).astype(BF16)


def _in_kernel(h_ref, g_ref, w_ref, kv_ref, *rest, splits, scales):
    out_refs, wb_sc = rest[:-1], rest[-1]
    _cast_weight_once(w_ref, wb_sc)
    _project(h_ref[...], g_ref, wb_sc, kv_ref, out_refs, splits, scales)


def _in_moe_kernel(pos_ref, pos_next_ref, h2_ref, ys_ref, g_ref, w_ref, kv_ref, h_ref, *rest, splits, scales):
    out_refs, (ybuf, sem, wb_sc) = rest[:-3], rest[-3:]
    _cast_weight_once(w_ref, wb_sc)
    h = h2_ref[...] + _gathered_tile(pos_ref, pos_next_ref, ys_ref, ybuf, sem)
    h_ref[...] = h
    _project(h, g_ref, wb_sc, kv_ref, out_refs, splits, scales)


def _in_proj(h, moe, g, w_all, w_idx, kv, layer, batch, splits, scales, dtypes, tm):
    t, d = h.shape
    n = w_all.shape[2]
    tiles_per_batch = (t // batch) // tm
    mem_len = kv.shape[2]
    row = lambda i: (i, 0)
    const = lambda i: (0, 0)
    out_shape = [jax.ShapeDtypeStruct((t, w), dt) for w, dt in zip(splits, dtypes)]
    out_shape.append(jax.ShapeDtypeStruct((t, MEM_WIDTH), BF16))
    out_specs = [pl.BlockSpec((tm, w), row) for w in splits]
    out_specs.append(pl.BlockSpec((tm, MEM_WIDTH), row))
    in_specs = [pl.BlockSpec((tm, d), row),
                pl.BlockSpec((1, d), const),
                pl.BlockSpec((1, d, n), lambda i: (w_idx, 0, 0), pipeline_mode=pl.Buffered(1)),
                pl.BlockSpec((1, 1, mem_len, 2 * MEM_WIDTH),
                             lambda i: (layer, i // tiles_per_batch, 0, 0))]
    if moe is None:
        return pl.pallas_call(
            functools.partial(_in_kernel, splits=splits, scales=scales),
            grid=(t // tm,), in_specs=in_specs, out_specs=out_specs, out_shape=out_shape,
            scratch_shapes=[pltpu.VMEM((d, n), BF16)],
            compiler_params=pltpu.CompilerParams(dimension_semantics=("arbitrary",)),
            name="in_proj",
        )(h, g, w_all, kv)
    ys, pos = moe
    pos3 = pos.reshape(t // tm, 1, tm)
    in_specs = _pos_specs(t // tm, tm) + [in_specs[0], pl.BlockSpec(memory_space=pl.ANY)] + in_specs[1:]
    return pl.pallas_call(
        functools.partial(_in_moe_kernel, splits=splits, scales=scales),
        grid=(t // tm,), in_specs=in_specs,
        out_specs=[pl.BlockSpec((tm, d), row)] + out_specs,
        out_shape=[jax.ShapeDtypeStruct((t, d), F32)] + out_shape,
        scratch_shapes=[pltpu.VMEM((2, tm * SUBLANES, LANES), F32), pltpu.SemaphoreType.DMA((2,)),
                        pltpu.VMEM((d, n), BF16)],
        compiler_params=pltpu.CompilerParams(dimension_semantics=("arbitrary",)),
        name="in_proj_moe_gather",
    )(pos3, pos3, h, ys, g, w_all, kv)


def _gelu_tanh(x):
    return 0.5 * x * (1.0 + jnp.tanh(math.sqrt(2.0 / math.pi) * (x + 0.044715 * (x * x * x))))


def _softplus(z):
    return jnp.maximum(z, 0.0) + jnp.log1p(jnp.exp(-jnp.abs(z)))


def _rglru_kernel(ux_ref, ug_ref, cw_ref, cb_ref, wa_ref, ba_ref, wx_ref, bx_ref, lam_ref,
                  o_ref, xpad_sc, h_sc):
    ts, c = ux_ref.shape
    pad = 8

    @pl.when(pl.program_id(1) == 0)
    def _():
        xpad_sc[0:pad, :] = jnp.zeros((pad, c), F32)
        h_sc[...] = jnp.zeros_like(h_sc)

    ux = ux_ref[...]
    xpad_sc[pad:pad + ts, :] = ux
    xc = cb_ref[...] + cw_ref[CONV_WIDTH - 1:CONV_WIDTH, :] * ux
    for j in range(CONV_WIDTH - 1):
        xc = xc + cw_ref[j:j + 1, :] * xpad_sc[pl.ds(pad - (CONV_WIDTH - 1) + j, ts), :]
    xpad_sc[0:pad, :] = ux[ts - pad:ts, :]

    xcb = xc.astype(BF16)
    r = jax.nn.sigmoid(_dot(xcb, wa_ref[...]) + ba_ref[...])
    ig = jax.nn.sigmoid(_dot(xcb, wx_ref[...]) + bx_ref[...])
    log_a = (-LRU_C) * r * _softplus(-lam_ref[...])
    a = jnp.exp(log_a)
    y = 1.0 - a * a
    b = jnp.where(y > 0.0, y * lax.rsqrt(y), 0.0) * (ig * xc)

    n_groups = ts // SUBLANES
    a = a.reshape(n_groups, SUBLANES, c)
    b = b.reshape(n_groups, SUBLANES, c)
    row = lax.broadcasted_iota(jnp.int32, a.shape, 1)
    k = 1
    while k < SUBLANES:
        keep = row >= k
        a_s = jnp.where(keep, pltpu.roll(a, k, 1), 1.0)
        b_s = jnp.where(keep, pltpu.roll(b, k, 1), 0.0)
        b = a * b_s + b
        a = a * a_s
        k *= 2
    h_in = h_sc[...]
    groups = []
    for g in range(n_groups):
        groups.append(b[g] + a[g] * h_in)
        h_in = groups[-1][SUBLANES - 1:SUBLANES, :]
    h_sc[...] = h_in
    o_ref[...] = (jnp.concatenate(groups, axis=0) * _gelu_tanh(ug_ref[...])).astype(o_ref.dtype)


def _rglru(ux, ug, cw, cb, wa, ba, wx, bx, lam, batch, ts):
    t, c = ux.shape
    nt = (t // batch) // ts
    row = lambda b, s: (b * nt + s, 0)
    const = lambda b, s: (0, 0)
    return pl.pallas_call(
        _rglru_kernel,
        grid=(batch, nt),
        in_specs=[pl.BlockSpec((ts, c), row), pl.BlockSpec((ts, c), row),
                  pl.BlockSpec((CONV_WIDTH, c), const), pl.BlockSpec((1, c), const),
                  pl.BlockSpec((c, c), const), pl.BlockSpec((1, c), const),
                  pl.BlockSpec((c, c), const), pl.BlockSpec((1, c), const),
                  pl.BlockSpec((1, c), const)],
        out_specs=pl.BlockSpec((ts, c), row),
        out_shape=jax.ShapeDtypeStruct((t, c), BF16),
        scratch_shapes=[pltpu.VMEM((ts + 8, c), F32), pltpu.VMEM((1, c), F32)],
        compiler_params=pltpu.CompilerParams(dimension_semantics=("arbitrary", "arbitrary")),
        name="rglru",
    )(ux, ug, cw, cb, wa, ba, wx, bx, lam)


def _block_diag(w):
    g, n, _ = w.shape
    eye = jnp.eye(g, dtype=w.dtype)
    return (eye[:, None, :, None] * w[:, :, None, :]).reshape(g * n, g * n)


def _bucket_table():
    i = np.arange(REL_MAX_DIST)[:, None]
    j = np.arange(REL_MAX_DIST)[None, :]

    def bucket(n):
        nf = np.maximum(n, 1).astype(np.float32)
        large = REL_MAX_EXACT + (np.log(nf / REL_MAX_EXACT) / math.log(REL_MAX_DIST / REL_MAX_EXACT)
                                 * (REL_BUCKETS - REL_MAX_EXACT)).astype(np.int32)
        return np.where(n < REL_MAX_EXACT, n, np.minimum(large, REL_BUCKETS - 1))

    diag = np.where(j <= i, bucket(np.maximum(i - j, 0)), -1)
    below = bucket(REL_MAX_DIST + i - j)
    return np.stack([diag, below]).astype(np.int32)


def _bias_kernel(rb_ref, bucket_ref, o_ref):
    h = pl.program_id(0)
    bk = bucket_ref[...]
    far = rb_ref[REL_BUCKETS - 1, h]
    acc = jnp.zeros(bk.shape, F32)
    for b in range(REL_BUCKETS - 1):
        acc = jnp.where(bk == b, rb_ref[b, h] - far, acc)
    blocks = jnp.where(bk < 0, NEG, acc * LOG2E)
    blk = REL_MAX_DIST
    tile = o_ref.shape[-1]
    o_ref[...] = jnp.zeros_like(o_ref)
    for a in range(tile // blk):
        rows = slice(a * blk, (a + 1) * blk)
        o_ref[0, 0, rows, rows] = blocks[0]
        if a >= 1:
            o_ref[0, 0, rows, (a - 1) * blk:a * blk] = blocks[1]
        if (a + 1) * blk < tile:
            o_ref[0, 0, rows, (a + 1) * blk:] = jnp.full((blk, tile - (a + 1) * blk), NEG, F32)
    o_ref[0, 1, 0:blk, tile - blk:] = blocks[1]


def _bias_tiles(rel_bias, tile):
    assert tile % REL_MAX_DIST == 0
    heads = rel_bias.shape[1]
    buckets = jnp.asarray(_bucket_table())
    return pl.pallas_call(
        _bias_kernel,
        grid=(heads,),
        in_specs=[pl.BlockSpec(memory_space=pltpu.SMEM),
                  pl.BlockSpec(buckets.shape, lambda h: (0, 0, 0))],
        out_specs=pl.BlockSpec((1, 2, tile, tile), lambda h: (h, 0, 0, 0)),
        out_shape=jax.ShapeDtypeStruct((heads, 2, tile, tile), F32),
        name="rel_bias_tiles",
    )(rel_bias, buckets)


def _attn_kernel(q_ref, k_ref, v_ref, bias_ref, lq1_ref, lk1_ref, lq2_ref, lk2_ref,
                 g_ref, o_ref, m_sc, acc_sc, sa_sc, sb_sc, pma_sc, pmb_sc, *, lambda_init):
    qi = pl.program_id(2)
    tile = q_ref.shape[1]
    reps = tile // LANES
    m_sc[...] = jnp.full(m_sc.shape, NEG, F32)
    acc_sc[...] = jnp.zeros_like(acc_sc)

    def lane_tile_max(s):
        parts = [s[:, c * LANES:(c + 1) * LANES] for c in range(s.shape[1] // LANES)]
        while len(parts) > 1:
            parts = [jnp.maximum(a, b) for a, b in zip(parts[::2], parts[1::2])]
        return parts[0]

    def scores(ki, bufs, bias_slot=None):
        s_ref, pm_ref = bufs
        start = pl.multiple_of(ki * tile, tile)
        q = q_ref[0]
        k = k_ref[0, pl.ds(start, tile), :]
        for j in range(2):
            lo, hi = j * DIFF_QK_DIM, (j + 1) * DIFF_QK_DIM
            s = _dot_nt(q[:, lo:hi], k[:, lo:hi])
            if bias_slot is not None:
                s = s + bias_ref[0, bias_slot]
            s_ref[j] = s
            pm_ref[j] = lane_tile_max(s)

    def accumulate(ki, bufs, bias_slot=None):
        s_ref, pm_ref = bufs
        start = pl.multiple_of(ki * tile, tile)
        v = v_ref[0, pl.ds(start, tile), :]
        v1 = jnp.concatenate([v, jnp.ones_like(v)], axis=-1)
        for j in range(2):
            s = s_ref[j]
            if bias_slot is not None:
                s = s + bias_ref[0, bias_slot]
                pm = lane_tile_max(s)
            else:
                pm = pm_ref[j]
            m_prev = m_sc[j]
            m_new = jnp.maximum(m_prev, jnp.max(pm, axis=-1, keepdims=True))
            alpha = jnp.exp2(m_prev - m_new)
            p = jnp.exp2(s - jnp.concatenate([m_new] * reps, axis=1))
            acc_sc[j] = jnp.concatenate([alpha, alpha], axis=1) * acc_sc[j] + _dot(p.astype(BF16), v1)
            m_sc[j] = m_new

    buf_a, buf_b = (sa_sc, pma_sc), (sb_sc, pmb_sc)
    n_far = jnp.maximum(qi - 1, 0)
    odd = n_far & 1
    pl.when(odd == 0)(lambda: scores(0, buf_a))

    @pl.when(odd == 1)
    def _():
        scores(0, buf_b)
        scores(1, buf_a)
        accumulate(0, buf_b)

    def far_pair(p, carry):
        base = odd + 2 * p
        scores(base + 1, buf_b)
        accumulate(base, buf_a)
        scores(base + 2, buf_a)
        accumulate(base + 1, buf_b)
        return carry

    lax.fori_loop(0, n_far >> 1, far_pair, 0)

    @pl.when(qi >= 1)
    def _():
        scores(qi, buf_b, bias_slot=0)
        accumulate(qi - 1, buf_a, bias_slot=1)
        accumulate(qi, buf_b)

    @pl.when(qi == 0)
    def _():
        accumulate(0, buf_a, bias_slot=0)

    lam = (jnp.exp(jnp.sum(lq1_ref[...] * lk1_ref[...], axis=-1, keepdims=True))
           - jnp.exp(jnp.sum(lq2_ref[...] * lk2_ref[...], axis=-1, keepdims=True))
           + lambda_init)
    dv = o_ref.shape[-1]
    o = (acc_sc[0, :, :dv] / acc_sc[0, :, dv:]) - lam * (acc_sc[1, :, :dv] / acc_sc[1, :, dv:])
    o_ref[0] = (_rms(o, g_ref[...]) * (1.0 - lambda_init)).astype(o_ref.dtype)


def _diff_attention(q, k, v, bias, lq1, lk1, lq2, lk2, g, lambda_init, tile):
    batch, s, width = q.shape
    heads = width // DIFF_V_DIM
    vec = lambda b, h, i: (0, 0)
    head_all = lambda b, h, i: (b, 0, h)
    return pl.pallas_call(
        functools.partial(_attn_kernel, lambda_init=lambda_init),
        grid=(batch, heads, s // tile),
        in_specs=[pl.BlockSpec((1, tile, DIFF_V_DIM), lambda b, h, i: (b, i, h)),
                  pl.BlockSpec((1, s, DIFF_V_DIM), head_all),
                  pl.BlockSpec((1, s, DIFF_V_DIM), head_all),
                  pl.BlockSpec((1, 2, tile, tile), lambda b, h, i: (h, 0, 0, 0)),
                  pl.BlockSpec((1, DIFF_QK_DIM), vec), pl.BlockSpec((1, DIFF_QK_DIM), vec),
                  pl.BlockSpec((1, DIFF_QK_DIM), vec), pl.BlockSpec((1, DIFF_QK_DIM), vec),
                  pl.BlockSpec((1, DIFF_V_DIM), vec)],
        out_specs=pl.BlockSpec((1, tile, DIFF_V_DIM), lambda b, h, i: (b, i, h)),
        out_shape=jax.ShapeDtypeStruct((batch, s, width), BF16),
        scratch_shapes=[pltpu.VMEM((2, tile, LANES), F32),
                        pltpu.VMEM((2, tile, 2 * DIFF_V_DIM), F32),
                        pltpu.VMEM((2, tile, tile), F32), pltpu.VMEM((2, tile, tile), F32),
                        pltpu.VMEM((2, tile, LANES), F32), pltpu.VMEM((2, tile, LANES), F32)],
        compiler_params=pltpu.CompilerParams(
            dimension_semantics=("arbitrary", "arbitrary", "arbitrary")),
        name="diff_attn",
    )(q, k, v, bias, lq1, lk1, lq2, lk2, g)


def _router_gates(logits, n_experts):
    lane = lax.broadcasted_iota(jnp.int32, logits.shape, 1)
    lane_f = lane.astype(F32)
    is_g = (lane >= n_experts) & (lane < n_experts + N_GROUPS)
    gl = jnp.where(is_g, logits, NEG)
    gmax = jnp.max(gl, axis=-1, keepdims=True)
    gidx = jnp.min(jnp.where(gl == gmax, lane_f - n_experts, 1e9), axis=-1, keepdims=True)
    gsum = jnp.sum(jnp.where(is_g, jnp.exp(gl - gmax), 0.0), axis=-1, keepdims=True)
    g_w = 1.0 / gsum
    in_grp = (lane < n_experts) & ((lane // EXPERTS_PER_GROUP).astype(F32) == gidx)
    el = jnp.where(in_grp, logits, NEG)
    t1 = jnp.max(el, axis=-1, keepdims=True)
    i1 = jnp.min(jnp.where(el == t1, lane_f, 1e9), axis=-1, keepdims=True)
    el2 = jnp.where(lane_f == i1, NEG, el)
    t2 = jnp.max(el2, axis=-1, keepdims=True)
    i2 = jnp.min(jnp.where(el2 == t2, lane_f, 1e9), axis=-1, keepdims=True)
    e2 = jnp.exp(t2 - t1)
    w1 = g_w / (1.0 + e2)
    w2 = g_w * e2 / (1.0 + e2)
    return jnp.where(lane_f == i1, w1, jnp.where(lane_f == i2, w2, 0.0)), gidx


def _out_kernel(tok_ref, mo_ref, h_ref, wo_ref, g_ref, wr_ref, h2_ref, hn_ref, route_ref, cnt_sc, wob_sc, *,
                n_experts):
    @pl.when(pl.program_id(0) == 0)
    def _():
        cnt_sc[...] = jnp.zeros_like(cnt_sc)

    _cast_weight_once(wo_ref, wob_sc)
    tw = tok_ref.shape[1]
    y = _dot(tok_ref[...], wob_sc[0:tw, :]) + _dot(mo_ref[...], wob_sc[tw:, :])
    h2 = h_ref[...] + y
    h2_ref[...] = h2
    hn = _rms(h2, g_ref[...])
    _to_token_tiles(hn_ref, hn)
    _, gidx = _router_gates(_dot(hn.astype(BF16), wr_ref[...]), n_experts)

    tm = h2.shape[0]
    lane = lax.broadcasted_iota(jnp.int32, (tm, LANES), 1)
    onehot = jnp.where(lane.astype(F32) == gidx, 1.0, 0.0)
    earlier = lax.broadcasted_iota(jnp.int32, (tm, tm), 1) < lax.broadcasted_iota(jnp.int32, (tm, tm), 0)
    before = _dot(jnp.where(earlier, 1.0, 0.0).astype(BF16), onehot.astype(BF16)) + cnt_sc[...]
    rank = jnp.sum(onehot * before, axis=-1, keepdims=True)
    cnt_sc[...] += jnp.sum(onehot, axis=0, keepdims=True)
    route = jnp.where(lane == 0, gidx, jnp.where(lane == 1, rank, 0.0))
    route_ref[...] = route[:, :route_ref.shape[1]]


def _out_proj(tok, mo, h, wo_all, layer, g, wr_bf16, n_experts, tm):
    t, d = h.shape
    tw, mw = tok.shape[1], mo.shape[1]
    row = lambda i: (i, 0)
    const = lambda i: (0, 0)
    return pl.pallas_call(
        functools.partial(_out_kernel, n_experts=n_experts),
        grid=(t // tm,),
        in_specs=[pl.BlockSpec((tm, tw), row), pl.BlockSpec((tm, mw), row), pl.BlockSpec((tm, d), row),
                  pl.BlockSpec((1, d, d), lambda i: (layer, 0, 0), pipeline_mode=pl.Buffered(1)),
                  pl.BlockSpec((1, d), const), pl.BlockSpec((d, LANES), const)],
        out_specs=[pl.BlockSpec((tm, d), row), pl.BlockSpec((tm * SUBLANES, LANES), row),
                   pl.BlockSpec((tm, ROUTE_WIDTH), row)],
        out_shape=[jax.ShapeDtypeStruct((t, d), F32), jax.ShapeDtypeStruct((t * SUBLANES, LANES), F32),
                   jax.ShapeDtypeStruct((t, ROUTE_WIDTH), F32)],
        scratch_shapes=[pltpu.VMEM((1, LANES), F32), pltpu.VMEM((d, d), BF16)],
        compiler_params=pltpu.CompilerParams(dimension_semantics=("arbitrary",)),
        name="out_proj_router",
    )(tok, mo, h, wo_all, g, wr_bf16)


def _route_tables(route, tile, n_tiles):
    gidx = route[:, 0].astype(jnp.int32)
    rank = route[:, 1].astype(jnp.int32)
    onehot = gidx[:, None] == jnp.arange(N_GROUPS, dtype=jnp.int32)[None, :]
    cnt = jnp.sum(onehot, axis=0, dtype=jnp.int32)
    ptiles = (cnt + tile - 1) // tile
    tile_end = jnp.cumsum(ptiles)
    off = (tile_end - ptiles) * tile
    pos = jnp.sum(jnp.where(onehot, off[None, :], 0), axis=1) + rank
    tiles = jnp.arange(n_tiles, dtype=jnp.int32)
    tile_group = jnp.minimum(jnp.sum(tiles[:, None] >= tile_end[None, :], axis=1), N_GROUPS - 1)
    return pos.astype(jnp.int32), tile_group.astype(jnp.int32), tile_end[-1:].astype(jnp.int32)


def _sort_kernel(pos_ref, x_ref, xs_in_ref, xs_ref, stage, sem):
    del xs_in_ref
    i = pl.program_id(0)
    last = pl.num_programs(0) - 1
    slot = i & 1
    n = x_ref.shape[0]

    def wait_slot(s):
        pltpu.make_async_copy(stage.at[s], xs_ref.at[pl.ds(0, n)], sem.at[s]).wait()

    pl.when(i >= 2)(lambda: wait_slot(slot))
    stage[slot] = x_ref[...]

    def make_copy(r):
        return pltpu.make_async_copy(_token_tile(stage.at[slot], r), _token_tile(xs_ref, pos_ref[0, 0, r]),
                                     sem.at[slot])

    _start_row_copies(n // SUBLANES, make_copy)

    @pl.when(i == last)
    def _():
        pl.when(i >= 1)(lambda: wait_slot(1 - slot))
        wait_slot(slot)


def _sort_rows(x, pos, n_tokens_out, tm):
    t = pos.shape[0]
    return pl.pallas_call(
        _sort_kernel,
        grid=(t // tm,),
        in_specs=[pl.BlockSpec((1, 1, tm), lambda i: (i, 0, 0), memory_space=pltpu.SMEM),
                  pl.BlockSpec((tm * SUBLANES, LANES), lambda i: (i, 0)),
                  pl.BlockSpec(memory_space=pl.ANY)],
        out_specs=pl.BlockSpec(memory_space=pl.ANY),
        out_shape=jax.ShapeDtypeStruct((n_tokens_out * SUBLANES, LANES), x.dtype),
        input_output_aliases={2: 0},
        scratch_shapes=[pltpu.VMEM((2, tm * SUBLANES, LANES), x.dtype), pltpu.SemaphoreType.DMA((2,))],
        compiler_params=pltpu.CompilerParams(dimension_semantics=("arbitrary",)),
        name="moe_sort_rows",
    )(pos.reshape(t // tm, 1, tm), x, jnp.zeros((n_tokens_out * SUBLANES, LANES), x.dtype))


def _experts_kernel(tg_ref, na_ref, xs_ref, wr_ref, w1_ref, w3_ref, w2_ref, ys_ref, w13b_sc, w2b_sc, *,
                    n_experts):
    i = pl.program_id(0)
    f = w2_ref.shape[2]

    @pl.when((i == 0) | (tg_ref[i] != tg_ref[jnp.maximum(i - 1, 0)]))
    def _():
        for j in range(EXPERTS_PER_GROUP):
            w13b_sc[j, :, 0:f] = w1_ref[0, j].astype(BF16)
            w13b_sc[j, :, f:2 * f] = w3_ref[0, j].astype(BF16)
            w2b_sc[j * f:(j + 1) * f, :] = w2_ref[0, j].astype(BF16)

    @pl.when(i >= na_ref[0])
    def _():
        ys_ref[...] = jnp.zeros_like(ys_ref)

    @pl.when(i < na_ref[0])
    def _():
        x = _from_token_tiles(xs_ref).astype(BF16)
        gates, _ = _router_gates(_dot(x, wr_ref[...]), n_experts)
        lane = lax.broadcasted_iota(jnp.int32, gates.shape, 1)
        first = tg_ref[i] * EXPERTS_PER_GROUP
        hid = []
        for j in range(EXPERTS_PER_GROUP):
            au = _dot(x, w13b_sc[j])
            a = au[:, :f]
            u = au[:, f:]
            gate = jnp.sum(jnp.where(lane == first + j, gates, 0.0), axis=-1, keepdims=True)
            hid.append(((a * jax.nn.sigmoid(a)) * u * gate).astype(BF16))
        _to_token_tiles(ys_ref, _dot(jnp.concatenate(hid, axis=-1), w2b_sc[...]))


def _experts(xs, tile_group, n_active, wr, w1_all, w3_all, w2_all, layer, n_experts, tile):
    f, d = w2_all.shape[2:]
    rows = xs.shape[0] // SUBLANES
    epg = EXPERTS_PER_GROUP
    group_w = lambda i, tg, na: (layer, tg[i], 0, 0)
    grid_spec = pltpu.PrefetchScalarGridSpec(
        num_scalar_prefetch=2,
        grid=(rows // tile,),
        in_specs=[pl.BlockSpec((tile * SUBLANES, LANES), lambda i, tg, na: (i, 0)),
                  pl.BlockSpec((d, LANES), lambda i, tg, na: (0, 0)),
                  pl.BlockSpec((1, epg, d, f), group_w),
                  pl.BlockSpec((1, epg, d, f), group_w),
                  pl.BlockSpec((1, epg, f, d), group_w)],
        out_specs=pl.BlockSpec((tile * SUBLANES, LANES), lambda i, tg, na: (i, 0)),
        scratch_shapes=[pltpu.VMEM((epg, d, 2 * f), BF16), pltpu.VMEM((epg * f, d), BF16)],
    )
    return pl.pallas_call(
        functools.partial(_experts_kernel, n_experts=n_experts),
        grid_spec=grid_spec,
        out_shape=jax.ShapeDtypeStruct(xs.shape, F32),
        compiler_params=pltpu.CompilerParams(dimension_semantics=("arbitrary",)),
        name="moe_experts",
    )(tile_group, n_active, xs, wr, w1_all, w3_all, w2_all)


def _final_kernel(pos_ref, pos_next_ref, h2_ref, ys_ref, g_ref, o_ref, ybuf, sem):
    o_ref[...] = _rms(h2_ref[...] + _gathered_tile(pos_ref, pos_next_ref, ys_ref, ybuf, sem), g_ref[...])


def _final_norm(h2, ys, pos, g, tm):
    t, d = h2.shape
    pos3 = pos.reshape(t // tm, 1, tm)
    return pl.pallas_call(
        _final_kernel,
        grid=(t // tm,),
        in_specs=_pos_specs(t // tm, tm) + [pl.BlockSpec((tm, d), lambda i: (i, 0)),
                                            pl.BlockSpec(memory_space=pl.ANY),
                                            pl.BlockSpec((1, d), lambda i: (0, 0))],
        out_specs=pl.BlockSpec((tm, d), lambda i: (i, 0)),
        out_shape=jax.ShapeDtypeStruct((t, d), F32),
        scratch_shapes=[pltpu.VMEM((2, tm * SUBLANES, LANES), F32), pltpu.SemaphoreType.DMA((2,))],
        compiler_params=pltpu.CompilerParams(dimension_semantics=("arbitrary",)),
        name="final_norm_moe_gather",
    )(pos3, pos3, h2, ys, g)


def kernel(x, mem, norm_mix_g, w_in_a, w_in_b, w_out, conv_w, conv_b, gate_a_w, gate_a_b, gate_x_w, gate_x_b, lru_lambda, diff_lq1, diff_lk1, diff_lq2, diff_lk2, diff_subln_g, rel_bias, mem_norm_g, w_mem_kv, norm_ffn_g, w_group, w_router, w_exp1, w_exp3, w_exp2, final_norm_g):
    batch, seq, d = x.shape
    depth = w_out.shape[0]
    n_experts = w_router.shape[-1]
    t = batch * seq
    rnn_width = conv_w.shape[-1]
    tok_width = d - MEM_WIDTH
    tm = min(512, seq)
    ts = min(256, seq)
    attn_tile = min(1024, seq)
    moe_tile = min(512, seq)
    moe_tiles = t // moe_tile + N_GROUPS

    kv = _memkv(mem.reshape(-1, d), mem_norm_g.reshape(1, d), w_mem_kv)
    kv = kv.reshape(depth, batch, mem.shape[1], 2 * MEM_WIDTH)
    bias = _bias_tiles(rel_bias, attn_tile)

    h = x.reshape(t, d)
    moe = None
    for i in range(depth):
        j = i // 2
        g_mix = norm_mix_g[i].reshape(1, d)
        if i % 2 == 0:
            outs = _in_proj(h, moe, g_mix, w_in_a, j, kv, i, batch,
                            (rnn_width, rnn_width), (1.0, 1.0), (F32, F32), tm)
            if moe is not None:
                h, outs = outs[0], outs[1:]
            ux, ug, mo = outs
            tok = _rglru(ux, ug, conv_w[j], conv_b[j].reshape(1, -1),
                         _block_diag(gate_a_w[j]).astype(BF16), gate_a_b[j].reshape(1, -1),
                         _block_diag(gate_x_w[j]).astype(BF16), gate_x_b[j].reshape(1, -1),
                         lru_lambda[j].reshape(1, -1), batch, ts)
        else:
            outs = _in_proj(h, moe, g_mix, w_in_b, j, kv, i, batch,
                            (tok_width, tok_width, tok_width),
                            (DIFF_QK_DIM ** -0.5 * LOG2E, 1.0, 1.0), (BF16, BF16, BF16), tm)
            if moe is not None:
                h, outs = outs[0], outs[1:]
            q, k, v, mo = outs
            lambda_init = 0.8 - 0.6 * math.exp(-0.3 * i)
            shp = (batch, seq, tok_width)
            tok = _diff_attention(q.reshape(shp), k.reshape(shp), v.reshape(shp), bias,
                                  diff_lq1[j].reshape(1, -1), diff_lk1[j].reshape(1, -1),
                                  diff_lq2[j].reshape(1, -1), diff_lk2[j].reshape(1, -1),
                                  diff_subln_g[j].reshape(1, -1), lambda_init, attn_tile)
            tok = tok.reshape(t, tok_width)
        wr = jnp.concatenate([w_router[i], w_group[i]], axis=-1)
        wr = jnp.pad(wr, ((0, 0), (0, LANES - wr.shape[1]))).astype(BF16)
        h, hn, route = _out_proj(tok, mo, h, w_out, i, norm_ffn_g[i].reshape(1, d), wr, n_experts, tm)
        pos, tile_group, n_active = _route_tables(route, moe_tile, moe_tiles)
        assert d == SUBLANES * LANES
        xs = _sort_rows(hn, pos, moe_tiles * moe_tile, tm)
        ys = _experts(xs, tile_group, n_active, wr, w_exp1, w_exp3, w_exp2, i, n_experts, moe_tile)
        moe = (ys, pos)
    return _final_norm(h, moe[0], moe[1], final_norm_g.reshape(1, d), tm).reshape(batch, seq, d)
```

```python
import functools
import math

import numpy as np
import jax
import jax.numpy as jnp
from jax import lax
from jax.experimental import pallas as pl
from jax.experimental.pallas import tpu as pltpu

F32 = jnp.float32
BF16 = jnp.bfloat16
EPS = 1e-6
NEG = -1e30
LOG2E = math.log2(math.e)

MEM_HEADS = 4
MEM_HEAD_DIM = 64
MEM_WIDTH = MEM_HEADS * MEM_HEAD_DIM
CONV_WIDTH = 4
LRU_C = 8.0
DIFF_QK_DIM = 64
DIFF_V_DIM = 2 * DIFF_QK_DIM
REL_BUCKETS = 32
REL_MAX_EXACT = 16
REL_MAX_DIST = 128
N_GROUPS = 4
EXPERTS_PER_GROUP = 4
LANES = 128
SUBLANES = 8
ROUTE_WIDTH = 8


def _rms(x, g):
    return x * lax.rsqrt(jnp.mean(x * x, axis=-1, keepdims=True) + EPS) * g


def _dot(a, b):
    return jnp.dot(a, b, preferred_element_type=F32)


def _dot_nt(a, b):
    return lax.dot_general(a, b, (((1,), (1,)), ((), ())), preferred_element_type=F32)


def _memkv_kernel(mem_ref, g_ref, w_ref, o_ref):
    xn = _rms(mem_ref[...], g_ref[...]).astype(BF16)
    o_ref[0] = _dot(xn, w_ref[0].astype(BF16)).astype(BF16)


def _memkv(mem2d, g, w):
    depth, d, n = w.shape
    rows = mem2d.shape[0]
    return pl.pallas_call(
        _memkv_kernel,
        grid=(depth,),
        in_specs=[pl.BlockSpec((rows, d), lambda i: (0, 0)),
                  pl.BlockSpec((1, d), lambda i: (0, 0)),
                  pl.BlockSpec((1, d, n), lambda i: (i, 0, 0))],
        out_specs=pl.BlockSpec((1, rows, n), lambda i: (i, 0, 0)),
        out_shape=jax.ShapeDtypeStruct((depth, rows, n), BF16),
        name="memkv",
    )(mem2d, g, w)


def _to_token_tiles(ref, x):
    n = x.shape[0]
    for c in range(SUBLANES):
        ref[pl.ds(c, n, stride=SUBLANES), :] = x[:, c * LANES:(c + 1) * LANES]


def _from_token_tiles(ref):
    n = ref.shape[0] // SUBLANES
    return jnp.concatenate([ref[pl.ds(c, n, stride=SUBLANES), :] for c in range(SUBLANES)], axis=1)


def _token_tile(ref, t):
    return ref.at[pl.ds(pl.multiple_of(t * SUBLANES, SUBLANES), SUBLANES)]


def _start_row_copies(n, make_copy):
    def body(p, carry):
        for u in range(2):
            make_copy(2 * p + u).start(priority=u)
        return carry

    lax.fori_loop(0, n // 2, body, 0, unroll=4)


def _start_gather(pos_ref, src_hbm, dst_vmem, sem):
    def make_copy(r):
        return pltpu.make_async_copy(_token_tile(src_hbm, pos_ref[0, 0, r]), _token_tile(dst_vmem, r), sem)

    _start_row_copies(dst_vmem.shape[0] // SUBLANES, make_copy)


def _wait_gather(src_hbm, dst_vmem, sem):
    pltpu.make_async_copy(src_hbm.at[pl.ds(0, dst_vmem.shape[0])], dst_vmem, sem).wait()


def _pos_specs(n_steps, tm):
    return [pl.BlockSpec((1, 1, tm), lambda i: (i, 0, 0), memory_space=pltpu.SMEM),
            pl.BlockSpec((1, 1, tm), lambda i: (jnp.minimum(i + 1, n_steps - 1), 0, 0),
                         memory_space=pltpu.SMEM)]


def _gathered_tile(pos_ref, pos_next_ref, src_hbm, buf, sem):
    i = pl.program_id(0)
    slot = i & 1
    pl.when(i == 0)(lambda: _start_gather(pos_ref, src_hbm, buf.at[0], sem.at[0]))

    @pl.when(i + 1 < pl.num_programs(0))
    def _():
        _start_gather(pos_next_ref, src_hbm, buf.at[1 - slot], sem.at[1 - slot])

    _wait_gather(src_hbm, buf.at[slot], sem.at[slot])
    return _from_token_tiles(buf.at[slot])


def _mem_attn(qm, kv):
    outs = []
    for hh in range(MEM_HEADS):
        lo, hi = hh * MEM_HEAD_DIM, (hh + 1) * MEM_HEAD_DIM
        q = (qm[:, lo:hi] * (MEM_HEAD_DIM ** -0.5)).astype(BF16)
        k = kv[:, lo:hi]
        v = kv[:, MEM_WIDTH + lo:MEM_WIDTH + hi]
        s = _dot_nt(q, k)
        m = jnp.max(s, axis=-1, keepdims=True)
        p = jnp.exp(s - m)
        l = jnp.sum(p, axis=-1, keepdims=True)
        outs.append(_dot(p.astype(BF16), v) / l)
    return jnp.concatenate(outs, axis=-1)


def _cast_weight_once(w_ref, wb_sc):
    @pl.when(pl.program_id(0) == 0)
    def _():
        wb_sc[...] = w_ref[0].astype(BF16)


def _project(h, g_ref, wb_sc, kv_ref, out_refs, splits, scales):
    hn = _rms(h, g_ref[...]).astype(BF16)
    proj = _dot(hn, wb_sc[...])
    off = 0
    for ref, width, scale in zip(out_refs[:-1], splits, scales):
        piece = proj[:, off:off + width]
        if scale != 1.0:
            piece = piece * scale
        ref[...] = piece.astype(ref.dtype)
        off += width
    out_refs[-1][...] = _mem_attn(proj[:, off:off + MEM_WIDTH], kv_ref[0, 0]).astype(BF16)


def _in_kernel(h_ref, g_ref, w_ref, kv_ref, *rest, splits, scales):
    out_refs, wb_sc = rest[:-1], rest[-1]
    _cast_weight_once(w_ref, wb_sc)
    _project(h_ref[...], g_ref, wb_sc, kv_ref, out_refs, splits, scales)


def _in_moe_kernel(pos_ref, pos_next_ref, h2_ref, ys_ref, g_ref, w_ref, kv_ref, h_ref, *rest, splits, scales):
    out_refs, (ybuf, sem, wb_sc) = rest[:-3], rest[-3:]
    _cast_weight_once(w_ref, wb_sc)
    h = h2_ref[...] + _gathered_tile(pos_ref, pos_next_ref, ys_ref, ybuf, sem)
    h_ref[...] = h
    _project(h, g_ref, wb_sc, kv_ref, out_refs, splits, scales)


def _in_proj(h, moe, g, w_all, w_idx, kv, layer, batch, splits, scales, dtypes, tm):
    t, d = h.shape
    n = w_all.shape[2]
    tiles_per_batch = (t // batch) // tm
    mem_len = kv.shape[2]
    row = lambda i: (i, 0)
    const = lambda i: (0, 0)
    out_shape = [jax.ShapeDtypeStruct((t, w), dt) for w, dt in zip(splits, dtypes)]
    out_shape.append(jax.ShapeDtypeStruct((t, MEM_WIDTH), BF16))
    out_specs = [pl.BlockSpec((tm, w), row) for w in splits]
    out_specs.append(pl.BlockSpec((tm, MEM_WIDTH), row))
    in_specs = [pl.BlockSpec((tm, d), row),
                pl.BlockSpec((1, d), const),
                pl.BlockSpec((1, d, n), lambda i: (w_idx, 0, 0), pipeline_mode=pl.Buffered(1)),
                pl.BlockSpec((1, 1, mem_len, 2 * MEM_WIDTH),
                             lambda i: (layer, i // tiles_per_batch, 0, 0))]
    if moe is None:
        return pl.pallas_call(
            functools.partial(_in_kernel, splits=splits, scales=scales),
            grid=(t // tm,), in_specs=in_specs, out_specs=out_specs, out_shape=out_shape,
            scratch_shapes=[pltpu.VMEM((d, n), BF16)],
            compiler_params=pltpu.CompilerParams(dimension_semantics=("arbitrary",)),
            name="in_proj",
        )(h, g, w_all, kv)
    ys, pos = moe
    pos3 = pos.reshape(t // tm, 1, tm)
    in_specs = _pos_specs(t // tm, tm) + [in_specs[0], pl.BlockSpec(memory_space=pl.ANY)] + in_specs[1:]
    return pl.pallas_call(
        functools.partial(_in_moe_kernel, splits=splits, scales=scales),
        grid=(t // tm,), in_specs=in_specs,
        out_specs=[pl.BlockSpec((tm, d), row)] + out_specs,
        out_shape=[jax.ShapeDtypeStruct((t, d), F32)] + out_shape,
        scratch_shapes=[pltpu.VMEM((2, tm * SUBLANES, LANES), F32), pltpu.SemaphoreType.DMA((2,)),
                        pltpu.VMEM((d, n), BF16)],
        compiler_params=pltpu.CompilerParams(dimension_semantics=("arbitrary",)),
        name="in_proj_moe_gather",
    )(pos3, pos3, h, ys, g, w_all, kv)


def _gelu_tanh(x):
    return 0.5 * x * (1.0 + jnp.tanh(math.sqrt(2.0 / math.pi) * (x + 0.044715 * (x * x * x))))


def _softplus(z):
    return jnp.maximum(z, 0.0) + jnp.log1p(jnp.exp(-jnp.abs(z)))


def _rglru_kernel(ux_ref, ug_ref, cw_ref, cb_ref, wa_ref, ba_ref, wx_ref, bx_ref, lam_ref,
                  o_ref, xpad_sc, h_sc):
    ts, c = ux_ref.shape
    pad = 8

    @pl.when(pl.program_id(1) == 0)
    def _():
        xpad_sc[0:pad, :] = jnp.zeros((pad, c), F32)
        h_sc[...] = jnp.zeros_like(h_sc)

    ux = ux_ref[...]
    xpad_sc[pad:pad + ts, :] = ux
    xc = cb_ref[...] + cw_ref[CONV_WIDTH - 1:CONV_WIDTH, :] * ux
    for j in range(CONV_WIDTH - 1):
        xc = xc + cw_ref[j:j + 1, :] * xpad_sc[pl.ds(pad - (CONV_WIDTH - 1) + j, ts), :]
    xpad_sc[0:pad, :] = ux[ts - pad:ts, :]

    xcb = xc.astype(BF16)
    r = jax.nn.sigmoid(_dot(xcb, wa_ref[...]) + ba_ref[...])
    ig = jax.nn.sigmoid(_dot(xcb, wx_ref[...]) + bx_ref[...])
    log_a = (-LRU_C) * r * _softplus(-lam_ref[...])
    a = jnp.exp(log_a)
    y = 1.0 - a * a
    b = jnp.where(y > 0.0, y * lax.rsqrt(y), 0.0) * (ig * xc)

    n_groups = ts // SUBLANES
    a = a.reshape(n_groups, SUBLANES, c)
    b = b.reshape(n_groups, SUBLANES, c)
    row = lax.broadcasted_iota(jnp.int32, a.shape, 1)
    k = 1
    while k < SUBLANES:
        keep = row >= k
        a_s = jnp.where(keep, pltpu.roll(a, k, 1), 1.0)
        b_s = jnp.where(keep, pltpu.roll(b, k, 1), 0.0)
        b = a * b_s + b
        a = a * a_s
        k *= 2
    h_in = h_sc[...]
    groups = []
    for g in range(n_groups):
        groups.append(b[g] + a[g] * h_in)
        h_in = groups[-1][SUBLANES - 1:SUBLANES, :]
    h_sc[...] = h_in
    o_ref[...] = (jnp.concatenate(groups, axis=0) * _gelu_tanh(ug_ref[...])).astype(o_ref.dtype)


def _rglru(ux, ug, cw, cb, wa, ba, wx, bx, lam, batch, ts):
    t, c = ux.shape
    nt = (t // batch) // ts
    row = lambda b, s: (b * nt + s, 0)
    const = lambda b, s: (0, 0)
    return pl.pallas_call(
        _rglru_kernel,
        grid=(batch, nt),
        in_specs=[pl.BlockSpec((ts, c), row), pl.BlockSpec((ts, c), row),
                  pl.BlockSpec((CONV_WIDTH, c), const), pl.BlockSpec((1, c), const),
                  pl.BlockSpec((c, c), const), pl.BlockSpec((1, c), const),
                  pl.BlockSpec((c, c), const), pl.BlockSpec((1, c), const),
                  pl.BlockSpec((1, c), const)],
        out_specs=pl.BlockSpec((ts, c), row),
        out_shape=jax.ShapeDtypeStruct((t, c), BF16),
        scratch_shapes=[pltpu.VMEM((ts + 8, c), F32), pltpu.VMEM((1, c), F32)],
        compiler_params=pltpu.CompilerParams(dimension_semantics=("arbitrary", "arbitrary")),
        name="rglru",
    )(ux, ug, cw, cb, wa, ba, wx, bx, lam)


def _block_diag(w):
    g, n, _ = w.shape
    eye = jnp.eye(g, dtype=w.dtype)
    return (eye[:, None, :, None] * w[:, :, None, :]).reshape(g * n, g * n)


def _bucket_table():
    i = np.arange(REL_MAX_DIST)[:, None]
    j = np.arange(REL_MAX_DIST)[None, :]

    def bucket(n):
        nf = np.maximum(n, 1).astype(np.float32)
        large = REL_MAX_EXACT + (np.log(nf / REL_MAX_EXACT) / math.log(REL_MAX_DIST / REL_MAX_EXACT)
                                 * (REL_BUCKETS - REL_MAX_EXACT)).astype(np.int32)
        return np.where(n < REL_MAX_EXACT, n, np.minimum(large, REL_BUCKETS - 1))

    diag = np.where(j <= i, bucket(np.maximum(i - j, 0)), -1)
    below = bucket(REL_MAX_DIST + i - j)
    return np.stack([diag, below]).astype(np.int32)


def _bias_kernel(rb_ref, bucket_ref, diag_ref, corner_ref):
    h = pl.program_id(0)
    bk = bucket_ref[...]
    far = rb_ref[REL_BUCKETS - 1, h]
    acc = jnp.zeros(bk.shape, F32)
    for b in range(REL_BUCKETS - 1):
        acc = jnp.where(bk == b, rb_ref[b, h] - far, acc)
    blocks = jnp.where(bk < 0, NEG, acc * LOG2E)
    blk = REL_MAX_DIST
    tile = diag_ref.shape[-1]
    diag_ref[...] = jnp.zeros_like(diag_ref)
    for a in range(tile // blk):
        rows = slice(a * blk, (a + 1) * blk)
        diag_ref[0, rows, rows] = blocks[0]
        if a >= 1:
            diag_ref[0, rows, (a - 1) * blk:a * blk] = blocks[1]
        if (a + 1) * blk < tile:
            diag_ref[0, rows, (a + 1) * blk:] = jnp.full((blk, tile - (a + 1) * blk), NEG, F32)
    corner_ref[0] = blocks[1]


def _bias_tiles(rel_bias, tile):
    assert tile % REL_MAX_DIST == 0
    heads = rel_bias.shape[1]
    blk = REL_MAX_DIST
    buckets = jnp.asarray(_bucket_table())
    return pl.pallas_call(
        _bias_kernel,
        grid=(heads,),
        in_specs=[pl.BlockSpec(memory_space=pltpu.SMEM),
                  pl.BlockSpec(buckets.shape, lambda h: (0, 0, 0))],
        out_specs=[pl.BlockSpec((1, tile, tile), lambda h: (h, 0, 0)),
                   pl.BlockSpec((1, blk, blk), lambda h: (h, 0, 0))],
        out_shape=[jax.ShapeDtypeStruct((heads, tile, tile), F32),
                   jax.ShapeDtypeStruct((heads, blk, blk), F32)],
        name="rel_bias_tiles",
    )(rel_bias, buckets)


def _attn_kernel(q_ref, qn_ref, k_ref, v_ref, bias_ref, corner_ref, lq1_ref, lk1_ref, lq2_ref, lk2_ref,
                 g_ref, o_ref, m_sc, acc_sc, sa_sc, sb_sc, sc_sc, pma_sc, pmb_sc, pmc_sc, *, lambda_init):
    qi = pl.program_id(2)
    tile = q_ref.shape[1]
    reps = tile // LANES
    m_sc[...] = jnp.full(m_sc.shape, NEG, F32)
    acc_sc[...] = jnp.zeros_like(acc_sc)

    def lane_tile_max(s):
        parts = [s[:, c * LANES:(c + 1) * LANES] for c in range(s.shape[1] // LANES)]
        while len(parts) > 1:
            parts = [jnp.maximum(a, b) for a, b in zip(parts[::2], parts[1::2])]
        return parts[0]

    def scores(ki, bufs, diag=False, q_from=q_ref):
        s_ref, pm_ref = bufs
        start = pl.multiple_of(ki * tile, tile)
        q = q_from[0]
        k = k_ref[0, pl.ds(start, tile), :]
        for j in range(2):
            lo, hi = j * DIFF_QK_DIM, (j + 1) * DIFF_QK_DIM
            s = _dot_nt(q[:, lo:hi], k[:, lo:hi])
            if diag:
                s = s + bias_ref[0]
            s_ref[j] = s
            pm_ref[j] = lane_tile_max(s)

    def add_corner_bias(bufs):
        s_ref, pm_ref = bufs
        blk = corner_ref.shape[-1]
        for j in range(2):
            s_ref[j, 0:blk, tile - blk:tile] = s_ref[j, 0:blk, tile - blk:tile] + corner_ref[0]
            pm_ref[j, 0:blk, :] = lane_tile_max(s_ref[j, 0:blk, :])

    def accumulate(ki, bufs, diag=False):
        s_ref, pm_ref = bufs
        start = pl.multiple_of(ki * tile, tile)
        v = v_ref[0, pl.ds(start, tile), :]
        v1 = jnp.concatenate([v, jnp.ones_like(v)], axis=-1)
        for j in range(2):
            s = s_ref[j]
            if diag:
                s = s + bias_ref[0]
                pm = lane_tile_max(s)
            else:
                pm = pm_ref[j]
            m_prev = m_sc[j]
            m_new = jnp.maximum(m_prev, jnp.max(pm, axis=-1, keepdims=True))
            alpha = jnp.exp2(m_prev - m_new)
            p = jnp.exp2(s - jnp.concatenate([m_new] * reps, axis=1))
            acc_sc[j] = jnp.concatenate([alpha, alpha], axis=1) * acc_sc[j] + _dot(p.astype(BF16), v1)
            m_sc[j] = m_new

    buf_a, buf_b, buf_c = (sa_sc, pma_sc), (sb_sc, pmb_sc), (sc_sc, pmc_sc)
    n_far = jnp.maximum(qi - 1, 0)
    odd = n_far & 1

    def score_next_step():
        scores(0, buf_c, q_from=qn_ref)

    @pl.when(qi == 0)
    def _():
        scores(0, buf_a)
        accumulate(0, buf_a, diag=True)
        score_next_step()

    @pl.when(qi == 1)
    def _():
        scores(1, buf_b, diag=True)
        add_corner_bias(buf_c)
        accumulate(0, buf_c)
        accumulate(1, buf_b)
        score_next_step()

    @pl.when((qi >= 2) & (odd == 1))
    def _():
        scores(1, buf_a)
        accumulate(0, buf_c)

    @pl.when((qi >= 2) & (odd == 0))
    def _():
        scores(1, buf_b)
        accumulate(0, buf_c)
        scores(2, buf_a)
        accumulate(1, buf_b)

    first = 2 - odd

    def far_pair(p, carry):
        base = first + 2 * p
        scores(base + 1, buf_b)
        accumulate(base, buf_a)
        scores(base + 2, buf_a)
        accumulate(base + 1, buf_b)
        return carry

    lax.fori_loop(0, jnp.maximum(n_far - first, 0) >> 1, far_pair, 0)

    @pl.when(qi >= 2)
    def _():
        scores(qi, buf_b, diag=True)
        add_corner_bias(buf_a)
        accumulate(qi - 1, buf_a)
        accumulate(qi, buf_b)
        score_next_step()

    lam = (jnp.exp(jnp.sum(lq1_ref[...] * lk1_ref[...], axis=-1, keepdims=True))
           - jnp.exp(jnp.sum(lq2_ref[...] * lk2_ref[...], axis=-1, keepdims=True))
           + lambda_init)
    dv = o_ref.shape[-1]
    o = (acc_sc[0, :, :dv] / acc_sc[0, :, dv:]) - lam * (acc_sc[1, :, :dv] / acc_sc[1, :, dv:])
    o_ref[0] = (_rms(o, g_ref[...]) * (1.0 - lambda_init)).astype(o_ref.dtype)


def _diff_attention(q, k, v, bias, corner, lq1, lk1, lq2, lk2, g, lambda_init, tile):
    batch, s, width = q.shape
    heads = width // DIFF_V_DIM
    n_q = s // tile
    vec = lambda b, h, i: (0, 0)
    head_all = lambda b, h, i: (b, 0, h)
    score_buf = pltpu.VMEM((2, tile, tile), F32)
    tile_max_buf = pltpu.VMEM((2, tile, LANES), F32)
    return pl.pallas_call(
        functools.partial(_attn_kernel, lambda_init=lambda_init),
        grid=(batch, heads, n_q),
        in_specs=[pl.BlockSpec((1, tile, DIFF_V_DIM), lambda b, h, i: (b, i, h)),
                  pl.BlockSpec((1, tile, DIFF_V_DIM), lambda b, h, i: (b, jnp.minimum(i + 1, n_q - 1), h)),
                  pl.BlockSpec((1, s, DIFF_V_DIM), head_all),
                  pl.BlockSpec((1, s, DIFF_V_DIM), head_all),
                  pl.BlockSpec((1, tile, tile), lambda b, h, i: (h, 0, 0)),
                  pl.BlockSpec((1,) + corner.shape[1:], lambda b, h, i: (h, 0, 0)),
                  pl.BlockSpec((1, DIFF_QK_DIM), vec), pl.BlockSpec((1, DIFF_QK_DIM), vec),
                  pl.BlockSpec((1, DIFF_QK_DIM), vec), pl.BlockSpec((1, DIFF_QK_DIM), vec),
                  pl.BlockSpec((1, DIFF_V_DIM), vec)],
        out_specs=pl.BlockSpec((1, tile, DIFF_V_DIM), lambda b, h, i: (b, i, h)),
        out_shape=jax.ShapeDtypeStruct((batch, s, width), BF16),
        scratch_shapes=[pltpu.VMEM((2, tile, LANES), F32),
                        pltpu.VMEM((2, tile, 2 * DIFF_V_DIM), F32),
                        score_buf, score_buf, score_buf, tile_max_buf, tile_max_buf, tile_max_buf],
        compiler_params=pltpu.CompilerParams(
            dimension_semantics=("arbitrary", "arbitrary", "arbitrary")),
        name="diff_attn",
    )(q, q, k, v, bias, corner, lq1, lk1, lq2, lk2, g)


def _router_gates(logits, n_experts):
    lane = lax.broadcasted_iota(jnp.int32, logits.shape, 1)
    lane_f = lane.astype(F32)
    is_g = (lane >= n_experts) & (lane < n_experts + N_GROUPS)
    gl = jnp.where(is_g, logits, NEG)
    gmax = jnp.max(gl, axis=-1, keepdims=True)
    gidx = jnp.min(jnp.where(gl == gmax, lane_f - n_experts, 1e9), axis=-1, keepdims=True)
    gsum = jnp.sum(jnp.where(is_g, jnp.exp(gl - gmax), 0.0), axis=-1, keepdims=True)
    g_w = 1.0 / gsum
    in_grp = (lane < n_experts) & ((lane // EXPERTS_PER_GROUP).astype(F32) == gidx)
    el = jnp.where(in_grp, logits, NEG)
    t1 = jnp.max(el, axis=-1, keepdims=True)
    i1 = jnp.min(jnp.where(el == t1, lane_f, 1e9), axis=-1, keepdims=True)
    el2 = jnp.where(lane_f == i1, NEG, el)
    t2 = jnp.max(el2, axis=-1, keepdims=True)
    i2 = jnp.min(jnp.where(el2 == t2, lane_f, 1e9), axis=-1, keepdims=True)
    e2 = jnp.exp(t2 - t1)
    w1 = g_w / (1.0 + e2)
    w2 = g_w * e2 / (1.0 + e2)
    return jnp.where(lane_f == i1, w1, jnp.where(lane_f == i2, w2, 0.0)), gidx


def _out_kernel(tok_ref, mo_ref, h_ref, wo_ref, g_ref, wr_ref, h2_ref, hn_ref, route_ref, cnt_sc, wob_sc, *,
                n_experts):
    @pl.when(pl.program_id(0) == 0)
    def _():
        cnt_sc[...] = jnp.zeros_like(cnt_sc)

    _cast_weight_once(wo_ref, wob_sc)
    tw = tok_ref.shape[1]
    y = _dot(tok_ref[...], wob_sc[0:tw, :]) + _dot(mo_ref[...], wob_sc[tw:, :])
    h2 = h_ref[...] + y
    h2_ref[...] = h2
    hn = _rms(h2, g_ref[...])
    _to_token_tiles(hn_ref, hn)
    _, gidx = _router_gates(_dot(hn.astype(BF16), wr_ref[...]), n_experts)

    tm = h2.shape[0]
    lane = lax.broadcasted_iota(jnp.int32, (tm, LANES), 1)
    onehot = jnp.where(lane.astype(F32) == gidx, 1.0, 0.0)
    earlier = lax.broadcasted_iota(jnp.int32, (tm, tm), 1) < lax.broadcasted_iota(jnp.int32, (tm, tm), 0)
    before = _dot(jnp.where(earlier, 1.0, 0.0).astype(BF16), onehot.astype(BF16)) + cnt_sc[...]
    rank = jnp.sum(onehot * before, axis=-1, keepdims=True)
    cnt_sc[...] += jnp.sum(onehot, axis=0, keepdims=True)
    route = jnp.where(lane == 0, gidx, jnp.where(lane == 1, rank, 0.0))
    route_ref[...] = route[:, :route_ref.shape[1]]


def _out_proj(tok, mo, h, wo_all, layer, g, wr_bf16, n_experts, tm):
    t, d = h.shape
    tw, mw = tok.shape[1], mo.shape[1]
    row = lambda i: (i, 0)
    const = lambda i: (0, 0)
    return pl.pallas_call(
        functools.partial(_out_kernel, n_experts=n_experts),
        grid=(t // tm,),
        in_specs=[pl.BlockSpec((tm, tw), row), pl.BlockSpec((tm, mw), row), pl.BlockSpec((tm, d), row),
                  pl.BlockSpec((1, d, d), lambda i: (layer, 0, 0), pipeline_mode=pl.Buffered(1)),
                  pl.BlockSpec((1, d), const), pl.BlockSpec((d, LANES), const)],
        out_specs=[pl.BlockSpec((tm, d), row), pl.BlockSpec((tm * SUBLANES, LANES), row),
                   pl.BlockSpec((tm, ROUTE_WIDTH), row)],
        out_shape=[jax.ShapeDtypeStruct((t, d), F32), jax.ShapeDtypeStruct((t * SUBLANES, LANES), F32),
                   jax.ShapeDtypeStruct((t, ROUTE_WIDTH), F32)],
        scratch_shapes=[pltpu.VMEM((1, LANES), F32), pltpu.VMEM((d, d), BF16)],
        compiler_params=pltpu.CompilerParams(dimension_semantics=("arbitrary",)),
        name="out_proj_router",
    )(tok, mo, h, wo_all, g, wr_bf16)


def _route_tables(route, tile, n_tiles):
    gidx = route[:, 0].astype(jnp.int32)
    rank = route[:, 1].astype(jnp.int32)
    onehot = gidx[:, None] == jnp.arange(N_GROUPS, dtype=jnp.int32)[None, :]
    cnt = jnp.sum(onehot, axis=0, dtype=jnp.int32)
    ptiles = (cnt + tile - 1) // tile
    tile_end = jnp.cumsum(ptiles)
    off = (tile_end - ptiles) * tile
    pos = jnp.sum(jnp.where(onehot, off[None, :], 0), axis=1) + rank
    tiles = jnp.arange(n_tiles, dtype=jnp.int32)
    tile_group = jnp.minimum(jnp.sum(tiles[:, None] >= tile_end[None, :], axis=1), N_GROUPS - 1)
    return pos.astype(jnp.int32), tile_group.astype(jnp.int32), tile_end[-1:].astype(jnp.int32)


def _sort_kernel(pos_ref, x_ref, xs_in_ref, xs_ref, stage, sem):
    del xs_in_ref
    i = pl.program_id(0)
    last = pl.num_programs(0) - 1
    slot = i & 1
    n = x_ref.shape[0]

    def wait_slot(s):
        pltpu.make_async_copy(stage.at[s], xs_ref.at[pl.ds(0, n)], sem.at[s]).wait()

    pl.when(i >= 2)(lambda: wait_slot(slot))
    stage[slot] = x_ref[...]

    def make_copy(r):
        return pltpu.make_async_copy(_token_tile(stage.at[slot], r), _token_tile(xs_ref, pos_ref[0, 0, r]),
                                     sem.at[slot])

    _start_row_copies(n // SUBLANES, make_copy)

    @pl.when(i == last)
    def _():
        pl.when(i >= 1)(lambda: wait_slot(1 - slot))
        wait_slot(slot)


def _sort_rows(x, pos, n_tokens_out, tm):
    t = pos.shape[0]
    return pl.pallas_call(
        _sort_kernel,
        grid=(t // tm,),
        in_specs=[pl.BlockSpec((1, 1, tm), lambda i: (i, 0, 0), memory_space=pltpu.SMEM),
                  pl.BlockSpec((tm * SUBLANES, LANES), lambda i: (i, 0)),
                  pl.BlockSpec(memory_space=pl.ANY)],
        out_specs=pl.BlockSpec(memory_space=pl.ANY),
        out_shape=jax.ShapeDtypeStruct((n_tokens_out * SUBLANES, LANES), x.dtype),
        input_output_aliases={2: 0},
        scratch_shapes=[pltpu.VMEM((2, tm * SUBLANES, LANES), x.dtype), pltpu.SemaphoreType.DMA((2,))],
        compiler_params=pltpu.CompilerParams(dimension_semantics=("arbitrary",)),
        name="moe_sort_rows",
    )(pos.reshape(t // tm, 1, tm), x, jnp.zeros((n_tokens_out * SUBLANES, LANES), x.dtype))


def _experts_kernel(tg_ref, na_ref, xs_ref, wr_ref, w1_ref, w3_ref, w2_ref, ys_ref, w13b_sc, w2b_sc, *,
                    n_experts):
    i = pl.program_id(0)
    f = w2_ref.shape[2]

    @pl.when((i == 0) | (tg_ref[i] != tg_ref[jnp.maximum(i - 1, 0)]))
    def _():
        for j in range(EXPERTS_PER_GROUP):
            w13b_sc[j, :, 0:f] = w1_ref[0, j].astype(BF16)
            w13b_sc[j, :, f:2 * f] = w3_ref[0, j].astype(BF16)
            w2b_sc[j * f:(j + 1) * f, :] = w2_ref[0, j].astype(BF16)

    @pl.when(i >= na_ref[0])
    def _():
        ys_ref[...] = jnp.zeros_like(ys_ref)

    @pl.when(i < na_ref[0])
    def _():
        x = _from_token_tiles(xs_ref).astype(BF16)
        gates, _ = _router_gates(_dot(x, wr_ref[...]), n_experts)
        lane = lax.broadcasted_iota(jnp.int32, gates.shape, 1)
        first = tg_ref[i] * EXPERTS_PER_GROUP
        hid = []
        for j in range(EXPERTS_PER_GROUP):
            au = _dot(x, w13b_sc[j])
            a = au[:, :f]
            u = au[:, f:]
            gate = jnp.sum(jnp.where(lane == first + j, gates, 0.0), axis=-1, keepdims=True)
            hid.append(((a * jax.nn.sigmoid(a)) * u * gate).astype(BF16))
        _to_token_tiles(ys_ref, _dot(jnp.concatenate(hid, axis=-1), w2b_sc[...]))


def _experts(xs, tile_group, n_active, wr, w1_all, w3_all, w2_all, layer, n_experts, tile):
    f, d = w2_all.shape[2:]
    rows = xs.shape[0] // SUBLANES
    epg = EXPERTS_PER_GROUP
    group_w = lambda i, tg, na: (layer, tg[i], 0, 0)
    grid_spec = pltpu.PrefetchScalarGridSpec(
        num_scalar_prefetch=2,
        grid=(rows // tile,),
        in_specs=[pl.BlockSpec((tile * SUBLANES, LANES), lambda i, tg, na: (i, 0)),
                  pl.BlockSpec((d, LANES), lambda i, tg, na: (0, 0)),
                  pl.BlockSpec((1, epg, d, f), group_w),
                  pl.BlockSpec((1, epg, d, f), group_w),
                  pl.BlockSpec((1, epg, f, d), group_w)],
        out_specs=pl.BlockSpec((tile * SUBLANES, LANES), lambda i, tg, na: (i, 0)),
        scratch_shapes=[pltpu.VMEM((epg, d, 2 * f), BF16), pltpu.VMEM((epg * f, d), BF16)],
    )
    return pl.pallas_call(
        functools.partial(_experts_kernel, n_experts=n_experts),
        grid_spec=grid_spec,
        out_shape=jax.ShapeDtypeStruct(xs.shape, F32),
        compiler_params=pltpu.CompilerParams(dimension_semantics=("arbitrary",)),
        name="moe_experts",
    )(tile_group, n_active, xs, wr, w1_all, w3_all, w2_all)


def _final_kernel(pos_ref, pos_next_ref, h2_ref, ys_ref, g_ref, o_ref, ybuf, sem):
    o_ref[...] = _rms(h2_ref[...] + _gathered_tile(pos_ref, pos_next_ref, ys_ref, ybuf, sem), g_ref[...])


def _final_norm(h2, ys, pos, g, tm):
    t, d = h2.shape
    pos3 = pos.reshape(t // tm, 1, tm)
    return pl.pallas_call(
        _final_kernel,
        grid=(t // tm,),
        in_specs=_pos_specs(t // tm, tm) + [pl.BlockSpec((tm, d), lambda i: (i, 0)),
                                            pl.BlockSpec(memory_space=pl.ANY),
                                            pl.BlockSpec((1, d), lambda i: (0, 0))],
        out_specs=pl.BlockSpec((tm, d), lambda i: (i, 0)),
        out_shape=jax.ShapeDtypeStruct((t, d), F32),
        scratch_shapes=[pltpu.VMEM((2, tm * SUBLANES, LANES), F32), pltpu.SemaphoreType.DMA((2,))],
        compiler_params=pltpu.CompilerParams(dimension_semantics=("arbitrary",)),
        name="final_norm_moe_gather",
    )(pos3, pos3, h2, ys, g)


def kernel(x, mem, norm_mix_g, w_in_a, w_in_b, w_out, conv_w, conv_b, gate_a_w, gate_a_b, gate_x_w, gate_x_b, lru_lambda, diff_lq1, diff_lk1, diff_lq2, diff_lk2, diff_subln_g, rel_bias, mem_norm_g, w_mem_kv, norm_ffn_g, w_group, w_router, w_exp1, w_exp3, w_exp2, final_norm_g):
    batch, seq, d = x.shape
    depth = w_out.shape[0]
    n_experts = w_router.shape[-1]
    t = batch * seq
    rnn_width = conv_w.shape[-1]
    tok_width = d - MEM_WIDTH
    tm = min(512, seq)
    ts = min(256, seq)
    attn_tile = min(1024, seq)
    moe_tile = min(512, seq)
    moe_tiles = t // moe_tile + N_GROUPS

    kv = _memkv(mem.reshape(-1, d), mem_norm_g.reshape(1, d), w_mem_kv)
    kv = kv.reshape(depth, batch, mem.shape[1], 2 * MEM_WIDTH)
    bias, corner = _bias_tiles(rel_bias, attn_tile)

    h = x.reshape(t, d)
    moe = None
    for i in range(depth):
        j = i // 2
        g_mix = norm_mix_g[i].reshape(1, d)
        if i % 2 == 0:
            outs = _in_proj(h, moe, g_mix, w_in_a, j, kv, i, batch,
                            (rnn_width, rnn_width), (1.0, 1.0), (F32, F32), tm)
            if moe is not None:
                h, outs = outs[0], outs[1:]
            ux, ug, mo = outs
            tok = _rglru(ux, ug, conv_w[j], conv_b[j].reshape(1, -1),
                         _block_diag(gate_a_w[j]).astype(BF16), gate_a_b[j].reshape(1, -1),
                         _block_diag(gate_x_w[j]).astype(BF16), gate_x_b[j].reshape(1, -1),
                         lru_lambda[j].reshape(1, -1), batch, ts)
        else:
            outs = _in_proj(h, moe, g_mix, w_in_b, j, kv, i, batch,
                            (tok_width, tok_width, tok_width),
                            (DIFF_QK_DIM ** -0.5 * LOG2E, 1.0, 1.0), (BF16, BF16, BF16), tm)
            if moe is not None:
                h, outs = outs[0], outs[1:]
            q, k, v, mo = outs
            lambda_init = 0.8 - 0.6 * math.exp(-0.3 * i)
            shp = (batch, seq, tok_width)
            tok = _diff_attention(q.reshape(shp), k.reshape(shp), v.reshape(shp), bias, corner,
                                  diff_lq1[j].reshape(1, -1), diff_lk1[j].reshape(1, -1),
                                  diff_lq2[j].reshape(1, -1), diff_lk2[j].reshape(1, -1),
                                  diff_subln_g[j].reshape(1, -1), lambda_init, attn_tile)
            tok = tok.reshape(t, tok_width)
        wr = jnp.concatenate([w_router[i], w_group[i]], axis=-1)
        wr = jnp.pad(wr, ((0, 0), (0, LANES - wr.shape[1]))).astype(BF16)
        h, hn, route = _out_proj(tok, mo, h, w_out, i, norm_ffn_g[i].reshape(1, d), wr, n_experts, tm)
        pos, tile_group, n_active = _route_tables(route, moe_tile, moe_tiles)
        assert d == SUBLANES * LANES
        xs = _sort_rows(hn, pos, moe_tiles * moe_tile, tm)
        ys = _experts(xs, tile_group, n_active, wr, w_exp1, w_exp3, w_exp2, i, n_experts, moe_tile)
        moe = (ys, pos)
    return _final_norm(h, moe[0], moe[1], final_norm_g.reshape(1, d), tm).reshape(batch, seq, d)
```

```python
import functools
import math

import numpy as np
import jax
import jax.numpy as jnp
from jax import lax
from jax.experimental import pallas as pl
from jax.experimental.pallas import tpu as pltpu

F32 = jnp.float32
BF16 = jnp.bfloat16
EPS = 1e-6
NEG = -1e30
LOG2E = math.log2(math.e)

MEM_HEADS = 4
MEM_HEAD_DIM = 64
MEM_WIDTH = MEM_HEADS * MEM_HEAD_DIM
CONV_WIDTH = 4
LRU_C = 8.0
DIFF_QK_DIM = 64
DIFF_V_DIM = 2 * DIFF_QK_DIM
REL_BUCKETS = 32
REL_MAX_EXACT = 16
REL_MAX_DIST = 128
N_GROUPS = 4
EXPERTS_PER_GROUP = 4
LANES = 128
SUBLANES = 8
ROUTE_WIDTH = 8


def _rms(x, g):
    return x * lax.rsqrt(jnp.mean(x * x, axis=-1, keepdims=True) + EPS) * g


def _dot(a, b):
    return jnp.dot(a, b, preferred_element_type=F32)


def _dot_nt(a, b):
    return lax.dot_general(a, b, (((1,), (1,)), ((), ())), preferred_element_type=F32)


def _memkv_kernel(mem_ref, g_ref, w_ref, o_ref):
    xn = _rms(mem_ref[...], g_ref[...]).astype(BF16)
    o_ref[0] = _dot(xn, w_ref[0].astype(BF16)).astype(BF16)


def _memkv(mem2d, g, w):
    depth, d, n = w.shape
    rows = mem2d.shape[0]
    return pl.pallas_call(
        _memkv_kernel,
        grid=(depth,),
        in_specs=[pl.BlockSpec((rows, d), lambda i: (0, 0)),
                  pl.BlockSpec((1, d), lambda i: (0, 0)),
                  pl.BlockSpec((1, d, n), lambda i: (i, 0, 0))],
        out_specs=pl.BlockSpec((1, rows, n), lambda i: (i, 0, 0)),
        out_shape=jax.ShapeDtypeStruct((depth, rows, n), BF16),
        name="memkv",
    )(mem2d, g, w)


def _to_token_tiles(ref, x):
    n = x.shape[0]
    for c in range(SUBLANES):
        ref[pl.ds(c, n, stride=SUBLANES), :] = x[:, c * LANES:(c + 1) * LANES]


def _from_token_tiles(ref):
    n = ref.shape[0] // SUBLANES
    return jnp.concatenate([ref[pl.ds(c, n, stride=SUBLANES), :] for c in range(SUBLANES)], axis=1)


def _token_tile(ref, t):
    return ref.at[pl.ds(pl.multiple_of(t * SUBLANES, SUBLANES), SUBLANES)]


def _start_row_copies(n, make_copy):
    def body(p, carry):
        for u in range(2):
            make_copy(2 * p + u).start(priority=u)
        return carry

    lax.fori_loop(0, n // 2, body, 0, unroll=4)


def _start_gather(pos_ref, src_hbm, dst_vmem, sem):
    def make_copy(r):
        return pltpu.make_async_copy(_token_tile(src_hbm, pos_ref[0, 0, r]), _token_tile(dst_vmem, r), sem)

    _start_row_copies(dst_vmem.shape[0] // SUBLANES, make_copy)


def _wait_gather(src_hbm, dst_vmem, sem):
    pltpu.make_async_copy(src_hbm.at[pl.ds(0, dst_vmem.shape[0])], dst_vmem, sem).wait()


def _pos_specs(n_steps, tm):
    return [pl.BlockSpec((1, 1, tm), lambda i: (i, 0, 0), memory_space=pltpu.SMEM),
            pl.BlockSpec((1, 1, tm), lambda i: (jnp.minimum(i + 1, n_steps - 1), 0, 0),
                         memory_space=pltpu.SMEM)]


def _gathered_tile(pos_ref, pos_next_ref, src_hbm, buf, sem):
    i = pl.program_id(0)
    slot = i & 1
    pl.when(i == 0)(lambda: _start_gather(pos_ref, src_hbm, buf.at[0], sem.at[0]))

    @pl.when(i + 1 < pl.num_programs(0))
    def _():
        _start_gather(pos_next_ref, src_hbm, buf.at[1 - slot], sem.at[1 - slot])

    _wait_gather(src_hbm, buf.at[slot], sem.at[slot])
    return _from_token_tiles(buf.at[slot])


def _mem_attn(qm, kv):
    outs = []
    for hh in range(MEM_HEADS):
        lo, hi = hh * MEM_HEAD_DIM, (hh + 1) * MEM_HEAD_DIM
        q = (qm[:, lo:hi] * (MEM_HEAD_DIM ** -0.5)).astype(BF16)
        k = kv[:, lo:hi]
        v = kv[:, MEM_WIDTH + lo:MEM_WIDTH + hi]
        s = _dot_nt(q, k)
        m = jnp.max(s, axis=-1, keepdims=True)
        p = jnp.exp(s - m)
        l = jnp.sum(p, axis=-1, keepdims=True)
        outs.append(_dot(p.astype(BF16), v) / l)
    return jnp.concatenate(outs, axis=-1)


def _cast_weight_once(w_ref, wb_sc):
    @pl.when(pl.program_id(0) == 0)
    def _():
        wb_sc[...] = w_ref[0].astype(BF16)


def _project(h, g_ref, wb_sc, kv_ref, out_refs, splits, scales):
    hn = _rms(h, g_ref[...]).astype(BF16)
    proj = _dot(hn, wb_sc[...])
    off = 0
    for ref, width, scale in zip(out_refs[:-1], splits, scales):
        piece = proj[:, off:off + width]
        if scale != 1.0:
            piece = piece * scale
        ref[...] = piece.astype(ref.dtype)
        off += width
    out_refs[-1][...] = _mem_attn(proj[:, off:off + MEM_WIDTH], kv_ref[0, 0]).astype(BF16)


def _in_kernel(h_ref, g_ref, w_ref, kv_ref, *rest, splits, scales):
    out_refs, wb_sc = rest[:-1], rest[-1]
    _cast_weight_once(w_ref, wb_sc)
    _project(h_ref[...], g_ref, wb_sc, kv_ref, out_refs, splits, scales)


def _in_moe_kernel(pos_ref, pos_next_ref, h2_ref, ys_ref, g_ref, w_ref, kv_ref, h_ref, *rest, splits, scales):
    out_refs, (ybuf, sem, wb_sc) = rest[:-3], rest[-3:]
    _cast_weight_once(w_ref, wb_sc)
    h = h2_ref[...] + _gathered_tile(pos_ref, pos_next_ref, ys_ref, ybuf, sem)
    h_ref[...] = h
    _project(h, g_ref, wb_sc, kv_ref, out_refs, splits, scales)


def _in_proj(h, moe, g, w_all, w_idx, kv, layer, batch, splits, scales, dtypes, tm):
    t, d = h.shape
    n = w_all.shape[2]
    tiles_per_batch = (t // batch) // tm
    mem_len = kv.shape[2]
    row = lambda i: (i, 0)
    const = lambda i: (0, 0)
    out_shape = [jax.ShapeDtypeStruct((t, w), dt) for w, dt in zip(splits, dtypes)]
    out_shape.append(jax.ShapeDtypeStruct((t, MEM_WIDTH), BF16))
    out_specs = [pl.BlockSpec((tm, w), row) for w in splits]
    out_specs.append(pl.BlockSpec((tm, MEM_WIDTH), row))
    in_specs = [pl.BlockSpec((tm, d), row),
                pl.BlockSpec((1, d), const),
                pl.BlockSpec((1, d, n), lambda i: (w_idx, 0, 0), pipeline_mode=pl.Buffered(1)),
                pl.BlockSpec((1, 1, mem_len, 2 * MEM_WIDTH),
                             lambda i: (layer, i // tiles_per_batch, 0, 0))]
    if moe is None:
        return pl.pallas_call(
            functools.partial(_in_kernel, splits=splits, scales=scales),
            grid=(t // tm,), in_specs=in_specs, out_specs=out_specs, out_shape=out_shape,
            scratch_shapes=[pltpu.VMEM((d, n), BF16)],
            compiler_params=pltpu.CompilerParams(dimension_semantics=("arbitrary",)),
            name="in_proj",
        )(h, g, w_all, kv)
    ys, pos = moe
    pos3 = pos.reshape(t // tm, 1, tm)
    in_specs = _pos_specs(t // tm, tm) + [in_specs[0], pl.BlockSpec(memory_space=pl.ANY)] + in_specs[1:]
    return pl.pallas_call(
        functools.partial(_in_moe_kernel, splits=splits, scales=scales),
        grid=(t // tm,), in_specs=in_specs,
        out_specs=[pl.BlockSpec((tm, d), row)] + out_specs,
        out_shape=[jax.ShapeDtypeStruct((t, d), F32)] + out_shape,
        scratch_shapes=[pltpu.VMEM((2, tm * SUBLANES, LANES), F32), pltpu.SemaphoreType.DMA((2,)),
                        pltpu.VMEM((d, n), BF16)],
        compiler_params=pltpu.CompilerParams(dimension_semantics=("arbitrary",)),
        name="in_proj_moe_gather",
    )(pos3, pos3, h, ys, g, w_all, kv)


def _gelu_tanh(x):
    return 0.5 * x * (1.0 + jnp.tanh(math.sqrt(2.0 / math.pi) * (x + 0.044715 * (x * x * x))))


def _softplus(z):
    return jnp.maximum(z, 0.0) + jnp.log1p(jnp.exp(-jnp.abs(z)))


def _rglru_kernel(ux_ref, ug_ref, cw_ref, cb_ref, wa_ref, ba_ref, wx_ref, bx_ref, lam_ref,
                  o_ref, xpad_sc, h_sc):
    ts, c = ux_ref.shape
    pad = 8

    @pl.when(pl.program_id(1) == 0)
    def _():
        xpad_sc[0:pad, :] = jnp.zeros((pad, c), F32)
        h_sc[...] = jnp.zeros_like(h_sc)

    ux = ux_ref[...]
    xpad_sc[pad:pad + ts, :] = ux
    xc = cb_ref[...] + cw_ref[CONV_WIDTH - 1:CONV_WIDTH, :] * ux
    for j in range(CONV_WIDTH - 1):
        xc = xc + cw_ref[j:j + 1, :] * xpad_sc[pl.ds(pad - (CONV_WIDTH - 1) + j, ts), :]
    xpad_sc[0:pad, :] = ux[ts - pad:ts, :]

    xcb = xc.astype(BF16)
    r = jax.nn.sigmoid(_dot(xcb, wa_ref[...]) + ba_ref[...])
    ig = jax.nn.sigmoid(_dot(xcb, wx_ref[...]) + bx_ref[...])
    log_a = (-LRU_C) * r * _softplus(-lam_ref[...])
    a = jnp.exp(log_a)
    y = 1.0 - a * a
    b = jnp.where(y > 0.0, y * lax.rsqrt(y), 0.0) * (ig * xc)

    n_groups = ts // SUBLANES
    a = a.reshape(n_groups, SUBLANES, c)
    b = b.reshape(n_groups, SUBLANES, c)
    row = lax.broadcasted_iota(jnp.int32, a.shape, 1)
    k = 1
    while k < SUBLANES:
        keep = row >= k
        a_s = jnp.where(keep, pltpu.roll(a, k, 1), 1.0)
        b_s = jnp.where(keep, pltpu.roll(b, k, 1), 0.0)
        b = a * b_s + b
        a = a * a_s
        k *= 2
    h_in = h_sc[...]
    groups = []
    for g in range(n_groups):
        groups.append(b[g] + a[g] * h_in)
        h_in = groups[-1][SUBLANES - 1:SUBLANES, :]
    h_sc[...] = h_in
    o_ref[...] = (jnp.concatenate(groups, axis=0) * _gelu_tanh(ug_ref[...])).astype(o_ref.dtype)


def _rglru(ux, ug, cw, cb, wa, ba, wx, bx, lam, batch, ts):
    t, c = ux.shape
    nt = (t // batch) // ts
    row = lambda b, s: (b * nt + s, 0)
    const = lambda b, s: (0, 0)
    return pl.pallas_call(
        _rglru_kernel,
        grid=(batch, nt),
        in_specs=[pl.BlockSpec((ts, c), row), pl.BlockSpec((ts, c), row),
                  pl.BlockSpec((CONV_WIDTH, c), const), pl.BlockSpec((1, c), const),
                  pl.BlockSpec((c, c), const), pl.BlockSpec((1, c), const),
                  pl.BlockSpec((c, c), const), pl.BlockSpec((1, c), const),
                  pl.BlockSpec((1, c), const)],
        out_specs=pl.BlockSpec((ts, c), row),
        out_shape=jax.ShapeDtypeStruct((t, c), BF16),
        scratch_shapes=[pltpu.VMEM((ts + 8, c), F32), pltpu.VMEM((1, c), F32)],
        compiler_params=pltpu.CompilerParams(dimension_semantics=("arbitrary", "arbitrary")),
        name="rglru",
    )(ux, ug, cw, cb, wa, ba, wx, bx, lam)


def _block_diag(w):
    g, n, _ = w.shape
    eye = jnp.eye(g, dtype=w.dtype)
    return (eye[:, None, :, None] * w[:, :, None, :]).reshape(g * n, g * n)


def _bucket_table():
    i = np.arange(REL_MAX_DIST)[:, None]
    j = np.arange(REL_MAX_DIST)[None, :]

    def bucket(n):
        nf = np.maximum(n, 1).astype(np.float32)
        large = REL_MAX_EXACT + (np.log(nf / REL_MAX_EXACT) / math.log(REL_MAX_DIST / REL_MAX_EXACT)
                                 * (REL_BUCKETS - REL_MAX_EXACT)).astype(np.int32)
        return np.where(n < REL_MAX_EXACT, n, np.minimum(large, REL_BUCKETS - 1))

    diag = np.where(j <= i, bucket(np.maximum(i - j, 0)), -1)
    below = bucket(REL_MAX_DIST + i - j)
    return np.stack([diag, below]).astype(np.int32)


def _bias_kernel(rb_ref, bucket_ref, diag_ref, corner_ref):
    h = pl.program_id(0)
    bk = bucket_ref[...]
    far = rb_ref[REL_BUCKETS - 1, h]
    acc = jnp.zeros(bk.shape, F32)
    for b in range(REL_BUCKETS - 1):
        acc = jnp.where(bk == b, rb_ref[b, h] - far, acc)
    blocks = jnp.where(bk < 0, NEG, acc * LOG2E)
    blk = REL_MAX_DIST
    tile = diag_ref.shape[-1]
    diag_ref[...] = jnp.zeros_like(diag_ref)
    for a in range(tile // blk):
        rows = slice(a * blk, (a + 1) * blk)
        diag_ref[0, rows, rows] = blocks[0]
        if a >= 1:
            diag_ref[0, rows, (a - 1) * blk:a * blk] = blocks[1]
        if (a + 1) * blk < tile:
            diag_ref[0, rows, (a + 1) * blk:] = jnp.full((blk, tile - (a + 1) * blk), NEG, F32)
    corner_ref[0] = blocks[1]


def _bias_tiles(rel_bias, tile):
    assert tile % REL_MAX_DIST == 0
    heads = rel_bias.shape[1]
    blk = REL_MAX_DIST
    buckets = jnp.asarray(_bucket_table())
    return pl.pallas_call(
        _bias_kernel,
        grid=(heads,),
        in_specs=[pl.BlockSpec(memory_space=pltpu.SMEM),
                  pl.BlockSpec(buckets.shape, lambda h: (0, 0, 0))],
        out_specs=[pl.BlockSpec((1, tile, tile), lambda h: (h, 0, 0)),
                   pl.BlockSpec((1, blk, blk), lambda h: (h, 0, 0))],
        out_shape=[jax.ShapeDtypeStruct((heads, tile, tile), F32),
                   jax.ShapeDtypeStruct((heads, blk, blk), F32)],
        name="rel_bias_tiles",
    )(rel_bias, buckets)


def _attn_kernel(q_ref, qn_ref, k_ref, v_ref, bias_ref, corner_ref, lq1_ref, lk1_ref, lq2_ref, lk2_ref,
                 g_ref, o_ref, m_sc, acc_sc, sa_sc, sb_sc, sc_sc, pma_sc, pmb_sc, pmc_sc, *, lambda_init):
    qi = pl.program_id(2)
    tile = q_ref.shape[1]
    reps = tile // LANES
    m_sc[...] = jnp.full(m_sc.shape, NEG, F32)
    acc_sc[...] = jnp.zeros_like(acc_sc)

    def lane_tile_max(s):
        parts = [s[:, c * LANES:(c + 1) * LANES] for c in range(s.shape[1] // LANES)]
        while len(parts) > 1:
            parts = [jnp.maximum(a, b) for a, b in zip(parts[::2], parts[1::2])]
        return parts[0]

    def scores(ki, bufs, diag=False, q_from=q_ref):
        s_ref, pm_ref = bufs
        start = pl.multiple_of(ki * tile, tile)
        q = q_from[0]
        k = k_ref[0, pl.ds(start, tile), :]
        for j in range(2):
            lo, hi = j * DIFF_QK_DIM, (j + 1) * DIFF_QK_DIM
            s = _dot_nt(q[:, lo:hi], k[:, lo:hi])
            if diag:
                s = s + bias_ref[0]
            s_ref[j] = s
            pm_ref[j] = lane_tile_max(s)

    def add_corner_bias(bufs):
        s_ref, pm_ref = bufs
        blk = corner_ref.shape[-1]
        for j in range(2):
            s_ref[j, 0:blk, tile - blk:tile] = s_ref[j, 0:blk, tile - blk:tile] + corner_ref[0]
            pm_ref[j, 0:blk, :] = lane_tile_max(s_ref[j, 0:blk, :])

    def accumulate(ki, bufs, diag=False):
        s_ref, pm_ref = bufs
        start = pl.multiple_of(ki * tile, tile)
        v = v_ref[0, pl.ds(start, tile), :]
        v1 = jnp.concatenate([v, jnp.ones_like(v)], axis=-1)
        for j in range(2):
            s = s_ref[j]
            if diag:
                s = s + bias_ref[0]
                pm = lane_tile_max(s)
            else:
                pm = pm_ref[j]
            m_prev = m_sc[j]
            m_new = jnp.maximum(m_prev, jnp.max(pm, axis=-1, keepdims=True))
            alpha = jnp.exp2(m_prev - m_new)
            p = jnp.exp2(s - jnp.concatenate([m_new] * reps, axis=1))
            acc_sc[j] = jnp.concatenate([alpha, alpha], axis=1) * acc_sc[j] + _dot(p.astype(BF16), v1)
            m_sc[j] = m_new

    buf_a, buf_b, buf_c = (sa_sc, pma_sc), (sb_sc, pmb_sc), (sc_sc, pmc_sc)
    n_far = jnp.maximum(qi - 1, 0)
    odd = n_far & 1

    def score_next_step():
        scores(0, buf_c, q_from=qn_ref)

    @pl.when(qi == 0)
    def _():
        scores(0, buf_a)
        accumulate(0, buf_a, diag=True)
        score_next_step()

    @pl.when(qi == 1)
    def _():
        scores(1, buf_b, diag=True)
        add_corner_bias(buf_c)
        accumulate(0, buf_c)
        accumulate(1, buf_b)
        score_next_step()

    @pl.when((qi >= 2) & (odd == 1))
    def _():
        scores(1, buf_a)
        accumulate(0, buf_c)

    @pl.when((qi >= 2) & (odd == 0))
    def _():
        scores(1, buf_b)
        accumulate(0, buf_c)
        scores(2, buf_a)
        accumulate(1, buf_b)

    first = 2 - odd

    def far_pair(p, carry):
        base = first + 2 * p
        scores(base + 1, buf_b)
        accumulate(base, buf_a)
        scores(base + 2, buf_a)
        accumulate(base + 1, buf_b)
        return carry

    lax.fori_loop(0, jnp.maximum(n_far - first, 0) >> 1, far_pair, 0)

    @pl.when(qi >= 2)
    def _():
        scores(qi, buf_b, diag=True)
        add_corner_bias(buf_a)
        accumulate(qi - 1, buf_a)
        accumulate(qi, buf_b)
        score_next_step()

    lam = (jnp.exp(jnp.sum(lq1_ref[...] * lk1_ref[...], axis=-1, keepdims=True))
           - jnp.exp(jnp.sum(lq2_ref[...] * lk2_ref[...], axis=-1, keepdims=True))
           + lambda_init)
    dv = o_ref.shape[-1]
    o = (acc_sc[0, :, :dv] / acc_sc[0, :, dv:]) - lam * (acc_sc[1, :, :dv] / acc_sc[1, :, dv:])
    o_ref[0] = (_rms(o, g_ref[...]) * (1.0 - lambda_init)).astype(o_ref.dtype)


def _diff_attention(q, k, v, bias, corner, lq1, lk1, lq2, lk2, g, lambda_init, tile):
    batch, s, width = q.shape
    heads = width // DIFF_V_DIM
    n_q = s // tile
    vec = lambda b, h, i: (0, 0)
    head_all = lambda b, h, i: (b, 0, h)
    score_buf = pltpu.VMEM((2, tile, tile), F32)
    tile_max_buf = pltpu.VMEM((2, tile, LANES), F32)
    return pl.pallas_call(
        functools.partial(_attn_kernel, lambda_init=lambda_init),
        grid=(batch, heads, n_q),
        in_specs=[pl.BlockSpec((1, tile, DIFF_V_DIM), lambda b, h, i: (b, i, h)),
                  pl.BlockSpec((1, tile, DIFF_V_DIM), lambda b, h, i: (b, jnp.minimum(i + 1, n_q - 1), h)),
                  pl.BlockSpec((1, s, DIFF_V_DIM), head_all),
                  pl.BlockSpec((1, s, DIFF_V_DIM), head_all),
                  pl.BlockSpec((1, tile, tile), lambda b, h, i: (h, 0, 0)),
                  pl.BlockSpec((1,) + corner.shape[1:], lambda b, h, i: (h, 0, 0)),
                  pl.BlockSpec((1, DIFF_QK_DIM), vec), pl.BlockSpec((1, DIFF_QK_DIM), vec),
                  pl.BlockSpec((1, DIFF_QK_DIM), vec), pl.BlockSpec((1, DIFF_QK_DIM), vec),
                  pl.BlockSpec((1, DIFF_V_DIM), vec)],
        out_specs=pl.BlockSpec((1, tile, DIFF_V_DIM), lambda b, h, i: (b, i, h)),
        out_shape=jax.ShapeDtypeStruct((batch, s, width), BF16),
        scratch_shapes=[pltpu.VMEM((2, tile, LANES), F32),
                        pltpu.VMEM((2, tile, 2 * DIFF_V_DIM), F32),
                        score_buf, score_buf, score_buf, tile_max_buf, tile_max_buf, tile_max_buf],
        compiler_params=pltpu.CompilerParams(
            dimension_semantics=("arbitrary", "arbitrary", "arbitrary")),
        name="diff_attn",
    )(q, q, k, v, bias, corner, lq1, lk1, lq2, lk2, g)


def _router_gates(logits, n_experts):
    lane = lax.broadcasted_iota(jnp.int32, logits.shape, 1)
    lane_f = lane.astype(F32)
    is_g = (lane >= n_experts) & (lane < n_experts + N_GROUPS)
    gl = jnp.where(is_g, logits, NEG)
    gmax = jnp.max(gl, axis=-1, keepdims=True)
    gidx = jnp.min(jnp.where(gl == gmax, lane_f - n_experts, 1e9), axis=-1, keepdims=True)
    gsum = jnp.sum(jnp.where(is_g, jnp.exp(gl - gmax), 0.0), axis=-1, keepdims=True)
    g_w = 1.0 / gsum
    in_grp = (lane < n_experts) & ((lane // EXPERTS_PER_GROUP).astype(F32) == gidx)
    el = jnp.where(in_grp, logits, NEG)
    t1 = jnp.max(el, axis=-1, keepdims=True)
    i1 = jnp.min(jnp.where(el == t1, lane_f, 1e9), axis=-1, keepdims=True)
    el2 = jnp.where(lane_f == i1, NEG, el)
    t2 = jnp.max(el2, axis=-1, keepdims=True)
    i2 = jnp.min(jnp.where(el2 == t2, lane_f, 1e9), axis=-1, keepdims=True)
    e2 = jnp.exp(t2 - t1)
    w1 = g_w / (1.0 + e2)
    w2 = g_w * e2 / (1.0 + e2)
    return jnp.where(lane_f == i1, w1, jnp.where(lane_f == i2, w2, 0.0)), gidx


def _out_kernel(tok_ref, mo_ref, h_ref, wo_ref, g_ref, wr_ref, h2_ref, hn_ref, route_ref, cnt_sc, wob_sc, *,
                n_experts):
    @pl.when(pl.program_id(0) == 0)
    def _():
        cnt_sc[...] = jnp.zeros_like(cnt_sc)

    _cast_weight_once(wo_ref, wob_sc)
    tw = tok_ref.shape[1]
    y = _dot(tok_ref[...], wob_sc[0:tw, :]) + _dot(mo_ref[...], wob_sc[tw:, :])
    h2 = h_ref[...] + y
    h2_ref[...] = h2
    hn = _rms(h2, g_ref[...])
    _to_token_tiles(hn_ref, hn)
    _, gidx = _router_gates(_dot(hn.astype(BF16), wr_ref[...]), n_experts)

    tm = h2.shape[0]
    lane = lax.broadcasted_iota(jnp.int32, (tm, LANES), 1)
    onehot = jnp.where(lane.astype(F32) == gidx, 1.0, 0.0)
    earlier = lax.broadcasted_iota(jnp.int32, (tm, tm), 1) < lax.broadcasted_iota(jnp.int32, (tm, tm), 0)
    before = _dot(jnp.where(earlier, 1.0, 0.0).astype(BF16), onehot.astype(BF16)) + cnt_sc[...]
    rank = jnp.sum(onehot * before, axis=-1, keepdims=True)
    cnt_sc[...] += jnp.sum(onehot, axis=0, keepdims=True)
    route = jnp.where(lane == 0, gidx, jnp.where(lane == 1, rank, 0.0))
    route_ref[...] = route[:, :route_ref.shape[1]]


def _out_proj(tok, mo, h, wo_all, layer, g, wr_bf16, n_experts, tm):
    t, d = h.shape
    tw, mw = tok.shape[1], mo.shape[1]
    row = lambda i: (i, 0)
    const = lambda i: (0, 0)
    return pl.pallas_call(
        functools.partial(_out_kernel, n_experts=n_experts),
        grid=(t // tm,),
        in_specs=[pl.BlockSpec((tm, tw), row), pl.BlockSpec((tm, mw), row), pl.BlockSpec((tm, d), row),
                  pl.BlockSpec((1, d, d), lambda i: (layer, 0, 0), pipeline_mode=pl.Buffered(1)),
                  pl.BlockSpec((1, d), const), pl.BlockSpec((d, LANES), const)],
        out_specs=[pl.BlockSpec((tm, d), row), pl.BlockSpec((tm * SUBLANES, LANES), row),
                   pl.BlockSpec((tm, ROUTE_WIDTH), row)],
        out_shape=[jax.ShapeDtypeStruct((t, d), F32), jax.ShapeDtypeStruct((t * SUBLANES, LANES), F32),
                   jax.ShapeDtypeStruct((t, ROUTE_WIDTH), F32)],
        scratch_shapes=[pltpu.VMEM((1, LANES), F32), pltpu.VMEM((d, d), BF16)],
        compiler_params=pltpu.CompilerParams(dimension_semantics=("arbitrary",)),
        name="out_proj_router",
    )(tok, mo, h, wo_all, g, wr_bf16)


def _route_tables(route, tile, n_tiles):
    gidx = route[:, 0].astype(jnp.int32)
    rank = route[:, 1].astype(jnp.int32)
    onehot = gidx[:, None] == jnp.arange(N_GROUPS, dtype=jnp.int32)[None, :]
    cnt = jnp.sum(onehot, axis=0, dtype=jnp.int32)
    ptiles = (cnt + tile - 1) // tile
    tile_end = jnp.cumsum(ptiles)
    off = (tile_end - ptiles) * tile
    pos = jnp.sum(jnp.where(onehot, off[None, :], 0), axis=1) + rank
    tiles = jnp.arange(n_tiles, dtype=jnp.int32)
    tile_group = jnp.minimum(jnp.sum(tiles[:, None] >= tile_end[None, :], axis=1), N_GROUPS - 1)
    return pos.astype(jnp.int32), tile_group.astype(jnp.int32), tile_end[-1:].astype(jnp.int32)


def _sort_kernel(pos_ref, x_ref, xs_in_ref, xs_ref, stage, sem):
    del xs_in_ref
    i = pl.program_id(0)
    last = pl.num_programs(0) - 1
    slot = i & 1
    n = x_ref.shape[0]

    def wait_slot(s):
        pltpu.make_async_copy(stage.at[s], xs_ref.at[pl.ds(0, n)], sem.at[s]).wait()

    pl.when(i >= 2)(lambda: wait_slot(slot))
    stage[slot] = x_ref[...]

    def make_copy(r):
        return pltpu.make_async_copy(_token_tile(stage.at[slot], r), _token_tile(xs_ref, pos_ref[0, 0, r]),
                                     sem.at[slot])

    _start_row_copies(n // SUBLANES, make_copy)

    @pl.when(i == last)
    def _():
        pl.when(i >= 1)(lambda: wait_slot(1 - slot))
        wait_slot(slot)


def _sort_rows(x, pos, n_tokens_out, tm):
    t = pos.shape[0]
    return pl.pallas_call(
        _sort_kernel,
        grid=(t // tm,),
        in_specs=[pl.BlockSpec((1, 1, tm), lambda i: (i, 0, 0), memory_space=pltpu.SMEM),
                  pl.BlockSpec((tm * SUBLANES, LANES), lambda i: (i, 0)),
                  pl.BlockSpec(memory_space=pl.ANY)],
        out_specs=pl.BlockSpec(memory_space=pl.ANY),
        out_shape=jax.ShapeDtypeStruct((n_tokens_out * SUBLANES, LANES), x.dtype),
        input_output_aliases={2: 0},
        scratch_shapes=[pltpu.VMEM((2, tm * SUBLANES, LANES), x.dtype), pltpu.SemaphoreType.DMA((2,))],
        compiler_params=pltpu.CompilerParams(dimension_semantics=("arbitrary",)),
        name="moe_sort_rows",
    )(pos.reshape(t // tm, 1, tm), x, jnp.zeros((n_tokens_out * SUBLANES, LANES), x.dtype))


def _experts_kernel(tg_ref, na_ref, xs_ref, wr_ref, w1_ref, w3_ref, w2_ref, ys_ref, w13b_sc, w2b_sc, *,
                    n_experts):
    i = pl.program_id(0)
    f = w2_ref.shape[2]

    @pl.when((i == 0) | (tg_ref[i] != tg_ref[jnp.maximum(i - 1, 0)]))
    def _():
        for j in range(EXPERTS_PER_GROUP):
            w13b_sc[j, :, 0:f] = w1_ref[0, j].astype(BF16)
            w13b_sc[j, :, f:2 * f] = w3_ref[0, j].astype(BF16)
            w2b_sc[j * f:(j + 1) * f, :] = w2_ref[0, j].astype(BF16)

    @pl.when(i >= na_ref[0])
    def _():
        ys_ref[...] = jnp.zeros_like(ys_ref)

    @pl.when(i < na_ref[0])
    def _():
        x = _from_token_tiles(xs_ref).astype(BF16)
        gates, _ = _router_gates(_dot(x, wr_ref[...]), n_experts)
        lane = lax.broadcasted_iota(jnp.int32, gates.shape, 1)
        first = tg_ref[i] * EXPERTS_PER_GROUP
        hid = []
        for j in range(EXPERTS_PER_GROUP):
            au = _dot(x, w13b_sc[j])
            a = au[:, :f]
            u = au[:, f:]
            gate = jnp.sum(jnp.where(lane == first + j, gates, 0.0), axis=-1, keepdims=True)
            hid.append(((a * jax.nn.sigmoid(a)) * u * gate).astype(BF16))
        _to_token_tiles(ys_ref, _dot(jnp.concatenate(hid, axis=-1), w2b_sc[...]))


def _experts(xs, tile_group, n_active, wr, w1_all, w3_all, w2_all, layer, n_experts, tile):
    f, d = w2_all.shape[2:]
    rows = xs.shape[0] // SUBLANES
    epg = EXPERTS_PER_GROUP
    group_w = lambda i, tg, na: (layer, tg[i], 0, 0)
    grid_spec = pltpu.PrefetchScalarGridSpec(
        num_scalar_prefetch=2,
        grid=(rows // tile,),
        in_specs=[pl.BlockSpec((tile * SUBLANES, LANES), lambda i, tg, na: (i, 0)),
                  pl.BlockSpec((d, LANES), lambda i, tg, na: (0, 0)),
                  pl.BlockSpec((1, epg, d, f), group_w),
                  pl.BlockSpec((1, epg, d, f), group_w),
                  pl.BlockSpec((1, epg, f, d), group_w)],
        out_specs=pl.BlockSpec((tile * SUBLANES, LANES), lambda i, tg, na: (i, 0)),
        scratch_shapes=[pltpu.VMEM((epg, d, 2 * f), BF16), pltpu.VMEM((epg * f, d), BF16)],
    )
    return pl.pallas_call(
        functools.partial(_experts_kernel, n_experts=n_experts),
        grid_spec=grid_spec,
        out_shape=jax.ShapeDtypeStruct(xs.shape, F32),
        compiler_params=pltpu.CompilerParams(dimension_semantics=("arbitrary",)),
        name="moe_experts",
    )(tile_group, n_active, xs, wr, w1_all, w3_all, w2_all)


def _final_kernel(pos_ref, pos_next_ref, h2_ref, ys_ref, g_ref, o_ref, ybuf, sem):
    o_ref[...] = _rms(h2_ref[...] + _gathered_tile(pos_ref, pos_next_ref, ys_ref, ybuf, sem), g_ref[...])


def _final_norm(h2, ys, pos, g, tm):
    t, d = h2.shape
    pos3 = pos.reshape(t // tm, 1, tm)
    return pl.pallas_call(
        _final_kernel,
        grid=(t // tm,),
        in_specs=_pos_specs(t // tm, tm) + [pl.BlockSpec((tm, d), lambda i: (i, 0)),
                                            pl.BlockSpec(memory_space=pl.ANY),
                                            pl.BlockSpec((1, d), lambda i: (0, 0))],
        out_specs=pl.BlockSpec((tm, d), lambda i: (i, 0)),
        out_shape=jax.ShapeDtypeStruct((t, d), F32),
        scratch_shapes=[pltpu.VMEM((2, tm * SUBLANES, LANES), F32), pltpu.SemaphoreType.DMA((2,))],
        compiler_params=pltpu.CompilerParams(dimension_semantics=("arbitrary",)),
        name="final_norm_moe_gather",
    )(pos3, pos3, h2, ys, g)


def kernel(x, mem, norm_mix_g, w_in_a, w_in_b, w_out, conv_w, conv_b, gate_a_w, gate_a_b, gate_x_w, gate_x_b, lru_lambda, diff_lq1, diff_lk1, diff_lq2, diff_lk2, diff_subln_g, rel_bias, mem_norm_g, w_mem_kv, norm_ffn_g, w_group, w_router, w_exp1, w_exp3, w_exp2, final_norm_g):
    batch, seq, d = x.shape
    depth = w_out.shape[0]
    n_experts = w_router.shape[-1]
    t = batch * seq
    rnn_width = conv_w.shape[-1]
    tok_width = d - MEM_WIDTH
    tm = min(512, seq)
    tm_wide = min(1024, seq)
    ts = min(512, seq)
    attn_tile = min(1024, seq)
    moe_tile = min(512, seq)
    moe_tiles = t // moe_tile + N_GROUPS

    kv = _memkv(mem.reshape(-1, d), mem_norm_g.reshape(1, d), w_mem_kv)
    kv = kv.reshape(depth, batch, mem.shape[1], 2 * MEM_WIDTH)
    bias, corner = _bias_tiles(rel_bias, attn_tile)

    h = x.reshape(t, d)
    moe = None
    for i in range(depth):
        j = i // 2
        g_mix = norm_mix_g[i].reshape(1, d)
        if i % 2 == 0:
            outs = _in_proj(h, moe, g_mix, w_in_a, j, kv, i, batch,
                            (rnn_width, rnn_width), (1.0, 1.0), (F32, F32), tm)
            if moe is not None:
                h, outs = outs[0], outs[1:]
            ux, ug, mo = outs
            tok = _rglru(ux, ug, conv_w[j], conv_b[j].reshape(1, -1),
                         _block_diag(gate_a_w[j]).astype(BF16), gate_a_b[j].reshape(1, -1),
                         _block_diag(gate_x_w[j]).astype(BF16), gate_x_b[j].reshape(1, -1),
                         lru_lambda[j].reshape(1, -1), batch, ts)
        else:
            outs = _in_proj(h, moe, g_mix, w_in_b, j, kv, i, batch,
                            (tok_width, tok_width, tok_width),
                            (DIFF_QK_DIM ** -0.5 * LOG2E, 1.0, 1.0), (BF16, BF16, BF16), tm)
            if moe is not None:
                h, outs = outs[0], outs[1:]
            q, k, v, mo = outs
            lambda_init = 0.8 - 0.6 * math.exp(-0.3 * i)
            shp = (batch, seq, tok_width)
            tok = _diff_attention(q.reshape(shp), k.reshape(shp), v.reshape(shp), bias, corner,
                                  diff_lq1[j].reshape(1, -1), diff_lk1[j].reshape(1, -1),
                                  diff_lq2[j].reshape(1, -1), diff_lk2[j].reshape(1, -1),
                                  diff_subln_g[j].reshape(1, -1), lambda_init, attn_tile)
            tok = tok.reshape(t, tok_width)
        wr = jnp.concatenate([w_router[i], w_group[i]], axis=-1)
        wr = jnp.pad(wr, ((0, 0), (0, LANES - wr.shape[1]))).astype(BF16)
        h, hn, route = _out_proj(tok, mo, h, w_out, i, norm_ffn_g[i].reshape(1, d), wr, n_experts, tm_wide)
        pos, tile_group, n_active = _route_tables(route, moe_tile, moe_tiles)
        assert d == SUBLANES * LANES
        xs = _sort_rows(hn, pos, moe_tiles * moe_tile, tm_wide)
        ys = _experts(xs, tile_group, n_active, wr, w_exp1, w_exp3, w_exp2, i, n_experts, moe_tile)
        moe = (ys, pos)
    return _final_norm(h, moe[0], moe[1], final_norm_g.reshape(1, d), tm_wide).reshape(batch, seq, d)
```

```python
import functools
import math

import numpy as np
import jax
import jax.numpy as jnp
from jax import lax
from jax.experimental import pallas as pl
from jax.experimental.pallas import tpu as pltpu

F32 = jnp.float32
BF16 = jnp.bfloat16
EPS = 1e-6
NEG = -1e30
LOG2E = math.log2(math.e)

MEM_HEADS = 4
MEM_HEAD_DIM = 64
MEM_WIDTH = MEM_HEADS * MEM_HEAD_DIM
CONV_WIDTH = 4
LRU_C = 8.0
DIFF_QK_DIM = 64
DIFF_V_DIM = 2 * DIFF_QK_DIM
REL_BUCKETS = 32
REL_MAX_EXACT = 16
REL_MAX_DIST = 128
N_GROUPS = 4
EXPERTS_PER_GROUP = 4
LANES = 128
SUBLANES = 8
ROUTE_WIDTH = 8


def _rms(x, g):
    return x * lax.rsqrt(jnp.mean(x * x, axis=-1, keepdims=True) + EPS) * g


def _dot(a, b):
    return jnp.dot(a, b, preferred_element_type=F32)


def _dot_nt(a, b):
    return lax.dot_general(a, b, (((1,), (1,)), ((), ())), preferred_element_type=F32)


def _memkv_kernel(mem_ref, g_ref, w_ref, o_ref):
    xn = _rms(mem_ref[...], g_ref[...]).astype(BF16)
    o_ref[0] = _dot(xn, w_ref[0].astype(BF16)).astype(BF16)


def _memkv(mem2d, g, w):
    depth, d, n = w.shape
    rows = mem2d.shape[0]
    return pl.pallas_call(
        _memkv_kernel,
        grid=(depth,),
        in_specs=[pl.BlockSpec((rows, d), lambda i: (0, 0)),
                  pl.BlockSpec((1, d), lambda i: (0, 0)),
                  pl.BlockSpec((1, d, n), lambda i: (i, 0, 0))],
        out_specs=pl.BlockSpec((1, rows, n), lambda i: (i, 0, 0)),
        out_shape=jax.ShapeDtypeStruct((depth, rows, n), BF16),
        name="memkv",
    )(mem2d, g, w)


def _to_token_tiles(ref, x):
    n = x.shape[0]
    for c in range(SUBLANES):
        ref[pl.ds(c, n, stride=SUBLANES), :] = x[:, c * LANES:(c + 1) * LANES]


def _from_token_tiles(ref):
    n = ref.shape[0] // SUBLANES
    return jnp.concatenate([ref[pl.ds(c, n, stride=SUBLANES), :] for c in range(SUBLANES)], axis=1)


def _token_tile(ref, t):
    return ref.at[pl.ds(pl.multiple_of(t * SUBLANES, SUBLANES), SUBLANES)]


def _start_row_copies(n, make_copy):
    def body(p, carry):
        for u in range(2):
            make_copy(2 * p + u).start(priority=u)
        return carry

    lax.fori_loop(0, n // 2, body, 0, unroll=4)


def _start_gather(pos_ref, src_hbm, dst_vmem, sem):
    def make_copy(r):
        return pltpu.make_async_copy(_token_tile(src_hbm, pos_ref[0, 0, r]), _token_tile(dst_vmem, r), sem)

    _start_row_copies(dst_vmem.shape[0] // SUBLANES, make_copy)


def _wait_gather(src_hbm, dst_vmem, sem):
    pltpu.make_async_copy(src_hbm.at[pl.ds(0, dst_vmem.shape[0])], dst_vmem, sem).wait()


def _pos_specs(n_steps, tm):
    return [pl.BlockSpec((1, 1, tm), lambda i: (i, 0, 0), memory_space=pltpu.SMEM),
            pl.BlockSpec((1, 1, tm), lambda i: (jnp.minimum(i + 1, n_steps - 1), 0, 0),
                         memory_space=pltpu.SMEM)]


def _gathered_tile(pos_ref, pos_next_ref, src_hbm, buf, sem):
    i = pl.program_id(0)
    slot = i & 1
    pl.when(i == 0)(lambda: _start_gather(pos_ref, src_hbm, buf.at[0], sem.at[0]))

    @pl.when(i + 1 < pl.num_programs(0))
    def _():
        _start_gather(pos_next_ref, src_hbm, buf.at[1 - slot], sem.at[1 - slot])

    _wait_gather(src_hbm, buf.at[slot], sem.at[slot])
    return _from_token_tiles(buf.at[slot])


def _mem_attn(qm, kv):
    outs = []
    for hh in range(MEM_HEADS):
        lo, hi = hh * MEM_HEAD_DIM, (hh + 1) * MEM_HEAD_DIM
        q = (qm[:, lo:hi] * (MEM_HEAD_DIM ** -0.5)).astype(BF16)
        k = kv[:, lo:hi]
        v = kv[:, MEM_WIDTH + lo:MEM_WIDTH + hi]
        s = _dot_nt(q, k)
        m = jnp.max(s, axis=-1, keepdims=True)
        p = jnp.exp(s - m)
        l = jnp.sum(p, axis=-1, keepdims=True)
        outs.append(_dot(p.astype(BF16), v) / l)
    return jnp.concatenate(outs, axis=-1)


def _cast_weight_once(w_ref, wb_sc):
    @pl.when(pl.program_id(0) == 0)
    def _():
        wb_sc[...] = w_ref[0].astype(BF16)


def _project(h, g_ref, wb_sc, kv_ref, out_refs, splits, scales):
    hn = _rms(h, g_ref[...]).astype(BF16)
    proj = _dot(hn, wb_sc[...])
    off = 0
    for ref, width, scale in zip(out_refs[:-1], splits, scales):
        piece = proj[:, off:off + width]
        if scale != 1.0:
            piece = piece * scale
        ref[...] = piece.astype(ref.dtype)
        off += width
    out_refs[-1][...] = _mem_attn(proj[:, off:off + MEM_WIDTH], kv_ref[0, 0]).astype(BF16)


def _in_kernel(h_ref, g_ref, w_ref, kv_ref, *rest, splits, scales):
    out_refs, wb_sc = rest[:-1], rest[-1]
    _cast_weight_once(w_ref, wb_sc)
    _project(h_ref[...], g_ref, wb_sc, kv_ref, out_refs, splits, scales)


def _in_moe_kernel(pos_ref, pos_next_ref, h2_ref, ys_ref, g_ref, w_ref, kv_ref, h_ref, *rest, splits, scales):
    out_refs, (ybuf, sem, wb_sc) = rest[:-3], rest[-3:]
    _cast_weight_once(w_ref, wb_sc)
    h = h2_ref[...] + _gathered_tile(pos_ref, pos_next_ref, ys_ref, ybuf, sem)
    h_ref[...] = h
    _project(h, g_ref, wb_sc, kv_ref, out_refs, splits, scales)


def _in_proj(h, moe, g, w_all, w_idx, kv, layer, batch, splits, scales, dtypes, tm):
    t, d = h.shape
    n = w_all.shape[2]
    tiles_per_batch = (t // batch) // tm
    mem_len = kv.shape[2]
    row = lambda i: (i, 0)
    const = lambda i: (0, 0)
    out_shape = [jax.ShapeDtypeStruct((t, w), dt) for w, dt in zip(splits, dtypes)]
    out_shape.append(jax.ShapeDtypeStruct((t, MEM_WIDTH), BF16))
    out_specs = [pl.BlockSpec((tm, w), row) for w in splits]
    out_specs.append(pl.BlockSpec((tm, MEM_WIDTH), row))
    in_specs = [pl.BlockSpec((tm, d), row),
                pl.BlockSpec((1, d), const),
                pl.BlockSpec((1, d, n), lambda i: (w_idx, 0, 0), pipeline_mode=pl.Buffered(1)),
                pl.BlockSpec((1, 1, mem_len, 2 * MEM_WIDTH),
                             lambda i: (layer, i // tiles_per_batch, 0, 0))]
    if moe is None:
        return pl.pallas_call(
            functools.partial(_in_kernel, splits=splits, scales=scales),
            grid=(t // tm,), in_specs=in_specs, out_specs=out_specs, out_shape=out_shape,
            scratch_shapes=[pltpu.VMEM((d, n), BF16)],
            compiler_params=pltpu.CompilerParams(dimension_semantics=("arbitrary",)),
            name="in_proj",
        )(h, g, w_all, kv)
    ys, pos = moe
    pos3 = pos.reshape(t // tm, 1, tm)
    in_specs = _pos_specs(t // tm, tm) + [in_specs[0], pl.BlockSpec(memory_space=pl.ANY)] + in_specs[1:]
    return pl.pallas_call(
        functools.partial(_in_moe_kernel, splits=splits, scales=scales),
        grid=(t // tm,), in_specs=in_specs,
        out_specs=[pl.BlockSpec((tm, d), row)] + out_specs,
        out_shape=[jax.ShapeDtypeStruct((t, d), F32)] + out_shape,
        scratch_shapes=[pltpu.VMEM((2, tm * SUBLANES, LANES), F32), pltpu.SemaphoreType.DMA((2,)),
                        pltpu.VMEM((d, n), BF16)],
        compiler_params=pltpu.CompilerParams(dimension_semantics=("arbitrary",)),
        name="in_proj_moe_gather",
    )(pos3, pos3, h, ys, g, w_all, kv)


def _gelu_tanh(x):
    return 0.5 * x * (1.0 + jnp.tanh(math.sqrt(2.0 / math.pi) * (x + 0.044715 * (x * x * x))))


def _softplus(z):
    return jnp.maximum(z, 0.0) + jnp.log1p(jnp.exp(-jnp.abs(z)))


def _rglru_kernel(ux_ref, ug_ref, cw_ref, cb_ref, wa_ref, ba_ref, wx_ref, bx_ref, lam_ref,
                  o_ref, xpad_sc, h_sc):
    ts, c = ux_ref.shape
    pad = 8

    @pl.when(pl.program_id(1) == 0)
    def _():
        xpad_sc[0:pad, :] = jnp.zeros((pad, c), F32)
        h_sc[...] = jnp.zeros_like(h_sc)

    ux = ux_ref[...]
    xpad_sc[pad:pad + ts, :] = ux
    xc = cb_ref[...] + cw_ref[CONV_WIDTH - 1:CONV_WIDTH, :] * ux
    for j in range(CONV_WIDTH - 1):
        xc = xc + cw_ref[j:j + 1, :] * xpad_sc[pl.ds(pad - (CONV_WIDTH - 1) + j, ts), :]
    xpad_sc[0:pad, :] = ux[ts - pad:ts, :]

    xcb = xc.astype(BF16)
    r = jax.nn.sigmoid(_dot(xcb, wa_ref[...]) + ba_ref[...])
    ig = jax.nn.sigmoid(_dot(xcb, wx_ref[...]) + bx_ref[...])
    log_a = (-LRU_C) * r * _softplus(-lam_ref[...])
    a = jnp.exp(log_a)
    y = 1.0 - a * a
    b = jnp.where(y > 0.0, y * lax.rsqrt(y), 0.0) * (ig * xc)

    n_groups = ts // SUBLANES
    a = a.reshape(n_groups, SUBLANES, c)
    b = b.reshape(n_groups, SUBLANES, c)
    row = lax.broadcasted_iota(jnp.int32, a.shape, 1)
    k = 1
    while k < SUBLANES:
        keep = row >= k
        a_s = jnp.where(keep, pltpu.roll(a, k, 1), 1.0)
        b_s = jnp.where(keep, pltpu.roll(b, k, 1), 0.0)
        b = a * b_s + b
        a = a * a_s
        k *= 2
    h_in = h_sc[...]
    groups = []
    for g in range(n_groups):
        groups.append(b[g] + a[g] * h_in)
        h_in = groups[-1][SUBLANES - 1:SUBLANES, :]
    h_sc[...] = h_in
    o_ref[...] = (jnp.concatenate(groups, axis=0) * _gelu_tanh(ug_ref[...])).astype(o_ref.dtype)


def _rglru(ux, ug, cw, cb, wa, ba, wx, bx, lam, batch, ts):
    t, c = ux.shape
    nt = (t // batch) // ts
    row = lambda b, s: (b * nt + s, 0)
    const = lambda b, s: (0, 0)
    return pl.pallas_call(
        _rglru_kernel,
        grid=(batch, nt),
        in_specs=[pl.BlockSpec((ts, c), row), pl.BlockSpec((ts, c), row),
                  pl.BlockSpec((CONV_WIDTH, c), const), pl.BlockSpec((1, c), const),
                  pl.BlockSpec((c, c), const), pl.BlockSpec((1, c), const),
                  pl.BlockSpec((c, c), const), pl.BlockSpec((1, c), const),
                  pl.BlockSpec((1, c), const)],
        out_specs=pl.BlockSpec((ts, c), row),
        out_shape=jax.ShapeDtypeStruct((t, c), BF16),
        scratch_shapes=[pltpu.VMEM((ts + 8, c), F32), pltpu.VMEM((1, c), F32)],
        compiler_params=pltpu.CompilerParams(dimension_semantics=("arbitrary", "arbitrary")),
        name="rglru",
    )(ux, ug, cw, cb, wa, ba, wx, bx, lam)


def _block_diag(w):
    g, n, _ = w.shape
    eye = jnp.eye(g, dtype=w.dtype)
    return (eye[:, None, :, None] * w[:, :, None, :]).reshape(g * n, g * n)


def _bucket_table():
    i = np.arange(REL_MAX_DIST)[:, None]
    j = np.arange(REL_MAX_DIST)[None, :]

    def bucket(n):
        nf = np.maximum(n, 1).astype(np.float32)
        large = REL_MAX_EXACT + (np.log(nf / REL_MAX_EXACT) / math.log(REL_MAX_DIST / REL_MAX_EXACT)
                                 * (REL_BUCKETS - REL_MAX_EXACT)).astype(np.int32)
        return np.where(n < REL_MAX_EXACT, n, np.minimum(large, REL_BUCKETS - 1))

    diag = np.where(j <= i, bucket(np.maximum(i - j, 0)), -1)
    below = bucket(REL_MAX_DIST + i - j)
    return np.stack([diag, below]).astype(np.int32)


def _bias_kernel(rb_ref, bucket_ref, diag_ref, corner_ref):
    h = pl.program_id(0)
    bk = bucket_ref[...]
    far = rb_ref[REL_BUCKETS - 1, h]
    acc = jnp.zeros(bk.shape, F32)
    for b in range(REL_BUCKETS - 1):
        acc = jnp.where(bk == b, rb_ref[b, h] - far, acc)
    blocks = jnp.where(bk < 0, NEG, acc * LOG2E)
    blk = REL_MAX_DIST
    tile = diag_ref.shape[-1]
    diag_ref[...] = jnp.zeros_like(diag_ref)
    for a in range(tile // blk):
        rows = slice(a * blk, (a + 1) * blk)
        diag_ref[0, rows, rows] = blocks[0]
        if a >= 1:
            diag_ref[0, rows, (a - 1) * blk:a * blk] = blocks[1]
        if (a + 1) * blk < tile:
            diag_ref[0, rows, (a + 1) * blk:] = jnp.full((blk, tile - (a + 1) * blk), NEG, F32)
    corner_ref[0] = blocks[1]


def _bias_tiles(rel_bias, tile):
    assert tile % REL_MAX_DIST == 0
    heads = rel_bias.shape[1]
    blk = REL_MAX_DIST
    buckets = jnp.asarray(_bucket_table())
    return pl.pallas_call(
        _bias_kernel,
        grid=(heads,),
        in_specs=[pl.BlockSpec(memory_space=pltpu.SMEM),
                  pl.BlockSpec(buckets.shape, lambda h: (0, 0, 0))],
        out_specs=[pl.BlockSpec((1, tile, tile), lambda h: (h, 0, 0)),
                   pl.BlockSpec((1, blk, blk), lambda h: (h, 0, 0))],
        out_shape=[jax.ShapeDtypeStruct((heads, tile, tile), F32),
                   jax.ShapeDtypeStruct((heads, blk, blk), F32)],
        name="rel_bias_tiles",
    )(rel_bias, buckets)


def _attn_kernel(q_ref, qn_ref, k_ref, v_ref, bias_ref, corner_ref, lq1_ref, lk1_ref, lq2_ref, lk2_ref,
                 g_ref, o_ref, m_sc, acc_sc, sa_sc, sb_sc, sc_sc, pma_sc, pmb_sc, pmc_sc, *, lambda_init):
    qi = pl.program_id(2)
    tile = q_ref.shape[1]
    reps = tile // LANES
    m_sc[...] = jnp.full(m_sc.shape, NEG, F32)
    acc_sc[...] = jnp.zeros_like(acc_sc)

    def lane_tile_max(s):
        parts = [s[:, c * LANES:(c + 1) * LANES] for c in range(s.shape[1] // LANES)]
        while len(parts) > 1:
            parts = [jnp.maximum(a, b) for a, b in zip(parts[::2], parts[1::2])]
        return parts[0]

    def scores(ki, bufs, diag=False, q_from=q_ref):
        s_ref, pm_ref = bufs
        start = pl.multiple_of(ki * tile, tile)
        q = q_from[0]
        k = k_ref[0, pl.ds(start, tile), :]
        for j in range(2):
            lo, hi = j * DIFF_QK_DIM, (j + 1) * DIFF_QK_DIM
            s = _dot_nt(q[:, lo:hi], k[:, lo:hi])
            if diag:
                s = s + bias_ref[0]
            s_ref[j] = s
            pm_ref[j] = lane_tile_max(s)

    def add_corner_bias(bufs):
        s_ref, pm_ref = bufs
        blk = corner_ref.shape[-1]
        for j in range(2):
            s_ref[j, 0:blk, tile - blk:tile] = s_ref[j, 0:blk, tile - blk:tile] + corner_ref[0]
            pm_ref[j, 0:blk, :] = lane_tile_max(s_ref[j, 0:blk, :])

    def accumulate(ki, bufs, diag=False):
        s_ref, pm_ref = bufs
        start = pl.multiple_of(ki * tile, tile)
        v = v_ref[0, pl.ds(start, tile), :]
        v1 = jnp.concatenate([v, jnp.ones_like(v)], axis=-1)
        for j in range(2):
            s = s_ref[j]
            if diag:
                s = s + bias_ref[0]
                pm = lane_tile_max(s)
            else:
                pm = pm_ref[j]
            m_prev = m_sc[j]
            m_new = jnp.maximum(m_prev, jnp.max(pm, axis=-1, keepdims=True))
            alpha = jnp.exp2(m_prev - m_new)
            p = jnp.exp2(s - jnp.concatenate([m_new] * reps, axis=1))
            acc_sc[j] = jnp.concatenate([alpha, alpha], axis=1) * acc_sc[j] + _dot(p.astype(BF16), v1)
            m_sc[j] = m_new

    buf_a, buf_b, buf_c = (sa_sc, pma_sc), (sb_sc, pmb_sc), (sc_sc, pmc_sc)
    n_far = jnp.maximum(qi - 1, 0)
    odd = n_far & 1

    def score_next_step():
        scores(0, buf_c, q_from=qn_ref)

    @pl.when(qi == 0)
    def _():
        scores(0, buf_a)
        accumulate(0, buf_a, diag=True)
        score_next_step()

    @pl.when(qi == 1)
    def _():
        scores(1, buf_b, diag=True)
        add_corner_bias(buf_c)
        accumulate(0, buf_c)
        accumulate(1, buf_b)
        score_next_step()

    @pl.when((qi >= 2) & (odd == 1))
    def _():
        scores(1, buf_a)
        accumulate(0, buf_c)

    @pl.when((qi >= 2) & (odd == 0))
    def _():
        scores(1, buf_b)
        accumulate(0, buf_c)
        scores(2, buf_a)
        accumulate(1, buf_b)

    first = 2 - odd

    def far_pair(p, carry):
        base = first + 2 * p
        scores(base + 1, buf_b)
        accumulate(base, buf_a)
        scores(base + 2, buf_a)
        accumulate(base + 1, buf_b)
        return carry

    lax.fori_loop(0, jnp.maximum(n_far - first, 0) >> 1, far_pair, 0)

    @pl.when(qi >= 2)
    def _():
        scores(qi, buf_b, diag=True)
        add_corner_bias(buf_a)
        accumulate(qi - 1, buf_a)
        accumulate(qi, buf_b)
        score_next_step()

    lam = (jnp.exp(jnp.sum(lq1_ref[...] * lk1_ref[...], axis=-1, keepdims=True))
           - jnp.exp(jnp.sum(lq2_ref[...] * lk2_ref[...], axis=-1, keepdims=True))
           + lambda_init)
    dv = o_ref.shape[-1]
    o = (acc_sc[0, :, :dv] / acc_sc[0, :, dv:]) - lam * (acc_sc[1, :, :dv] / acc_sc[1, :, dv:])
    o_ref[0] = (_rms(o, g_ref[...]) * (1.0 - lambda_init)).astype(o_ref.dtype)


def _diff_attention(q, k, v, bias, corner, lq1, lk1, lq2, lk2, g, lambda_init, tile):
    batch, s, width = q.shape
    heads = width // DIFF_V_DIM
    n_q = s // tile
    vec = lambda b, h, i: (0, 0)
    head_all = lambda b, h, i: (b, 0, h)
    score_buf = pltpu.VMEM((2, tile, tile), F32)
    tile_max_buf = pltpu.VMEM((2, tile, LANES), F32)
    return pl.pallas_call(
        functools.partial(_attn_kernel, lambda_init=lambda_init),
        grid=(batch, heads, n_q),
        in_specs=[pl.BlockSpec((1, tile, DIFF_V_DIM), lambda b, h, i: (b, i, h)),
                  pl.BlockSpec((1, tile, DIFF_V_DIM), lambda b, h, i: (b, jnp.minimum(i + 1, n_q - 1), h)),
                  pl.BlockSpec((1, s, DIFF_V_DIM), head_all),
                  pl.BlockSpec((1, s, DIFF_V_DIM), head_all),
                  pl.BlockSpec((1, tile, tile), lambda b, h, i: (h, 0, 0)),
                  pl.BlockSpec((1,) + corner.shape[1:], lambda b, h, i: (h, 0, 0)),
                  pl.BlockSpec((1, DIFF_QK_DIM), vec), pl.BlockSpec((1, DIFF_QK_DIM), vec),
                  pl.BlockSpec((1, DIFF_QK_DIM), vec), pl.BlockSpec((1, DIFF_QK_DIM), vec),
                  pl.BlockSpec((1, DIFF_V_DIM), vec)],
        out_specs=pl.BlockSpec((1, tile, DIFF_V_DIM), lambda b, h, i: (b, i, h)),
        out_shape=jax.ShapeDtypeStruct((batch, s, width), BF16),
        scratch_shapes=[pltpu.VMEM((2, tile, LANES), F32),
                        pltpu.VMEM((2, tile, 2 * DIFF_V_DIM), F32),
                        score_buf, score_buf, score_buf, tile_max_buf, tile_max_buf, tile_max_buf],
        compiler_params=pltpu.CompilerParams(
            dimension_semantics=("arbitrary", "arbitrary", "arbitrary")),
        name="diff_attn",
    )(q, q, k, v, bias, corner, lq1, lk1, lq2, lk2, g)


def _router_gates(logits, n_experts):
    lane = lax.broadcasted_iota(jnp.int32, logits.shape, 1)
    lane_f = lane.astype(F32)
    is_g = (lane >= n_experts) & (lane < n_experts + N_GROUPS)
    gl = jnp.where(is_g, logits, NEG)
    gmax = jnp.max(gl, axis=-1, keepdims=True)
    gidx = jnp.min(jnp.where(gl == gmax, lane_f - n_experts, 1e9), axis=-1, keepdims=True)
    gsum = jnp.sum(jnp.where(is_g, jnp.exp(gl - gmax), 0.0), axis=-1, keepdims=True)
    g_w = 1.0 / gsum
    in_grp = (lane < n_experts) & ((lane // EXPERTS_PER_GROUP).astype(F32) == gidx)
    el = jnp.where(in_grp, logits, NEG)
    t1 = jnp.max(el, axis=-1, keepdims=True)
    i1 = jnp.min(jnp.where(el == t1, lane_f, 1e9), axis=-1, keepdims=True)
    el2 = jnp.where(lane_f == i1, NEG, el)
    t2 = jnp.max(el2, axis=-1, keepdims=True)
    i2 = jnp.min(jnp.where(el2 == t2, lane_f, 1e9), axis=-1, keepdims=True)
    e2 = jnp.exp(t2 - t1)
    w1 = g_w / (1.0 + e2)
    w2 = g_w * e2 / (1.0 + e2)
    return jnp.where(lane_f == i1, w1, jnp.where(lane_f == i2, w2, 0.0)), gidx


def _out_kernel(tok_ref, mo_ref, h_ref, wo_ref, g_ref, wr_ref, h2_ref, hn_ref, route_ref, cnt_sc, wob_sc, *,
                n_experts):
    @pl.when(pl.program_id(0) == 0)
    def _():
        cnt_sc[...] = jnp.zeros_like(cnt_sc)

    _cast_weight_once(wo_ref, wob_sc)
    tw = tok_ref.shape[1]
    y = _dot(tok_ref[...], wob_sc[0:tw, :]) + _dot(mo_ref[...], wob_sc[tw:, :])
    h2 = h_ref[...] + y
    h2_ref[...] = h2
    hn = _rms(h2, g_ref[...])
    _to_token_tiles(hn_ref, hn)
    _, gidx = _router_gates(_dot(hn.astype(BF16), wr_ref[...]), n_experts)

    tm = h2.shape[0]
    lane = lax.broadcasted_iota(jnp.int32, (tm, LANES), 1)
    onehot = jnp.where(lane.astype(F32) == gidx, 1.0, 0.0)
    earlier = lax.broadcasted_iota(jnp.int32, (tm, tm), 1) < lax.broadcasted_iota(jnp.int32, (tm, tm), 0)
    before = _dot(jnp.where(earlier, 1.0, 0.0).astype(BF16), onehot.astype(BF16)) + cnt_sc[...]
    rank = jnp.sum(onehot * before, axis=-1, keepdims=True)
    cnt_sc[...] += jnp.sum(onehot, axis=0, keepdims=True)
    route = jnp.where(lane == 0, gidx, jnp.where(lane == 1, rank, 0.0))
    route_ref[...] = route[:, :route_ref.shape[1]]


def _out_proj(tok, mo, h, wo_all, layer, g, wr_bf16, n_experts, tm):
    t, d = h.shape
    tw, mw = tok.shape[1], mo.shape[1]
    row = lambda i: (i, 0)
    const = lambda i: (0, 0)
    return pl.pallas_call(
        functools.partial(_out_kernel, n_experts=n_experts),
        grid=(t // tm,),
        in_specs=[pl.BlockSpec((tm, tw), row), pl.BlockSpec((tm, mw), row), pl.BlockSpec((tm, d), row),
                  pl.BlockSpec((1, d, d), lambda i: (layer, 0, 0), pipeline_mode=pl.Buffered(1)),
                  pl.BlockSpec((1, d), const), pl.BlockSpec((d, LANES), const)],
        out_specs=[pl.BlockSpec((tm, d), row), pl.BlockSpec((tm * SUBLANES, LANES), row),
                   pl.BlockSpec((tm, ROUTE_WIDTH), row)],
        out_shape=[jax.ShapeDtypeStruct((t, d), F32), jax.ShapeDtypeStruct((t * SUBLANES, LANES), F32),
                   jax.ShapeDtypeStruct((t, ROUTE_WIDTH), F32)],
        scratch_shapes=[pltpu.VMEM((1, LANES), F32), pltpu.VMEM((d, d), BF16)],
        compiler_params=pltpu.CompilerParams(dimension_semantics=("arbitrary",)),
        name="out_proj_router",
    )(tok, mo, h, wo_all, g, wr_bf16)


def _route_tables(route, tile, n_tiles):
    gidx = route[:, 0].astype(jnp.int32)
    rank = route[:, 1].astype(jnp.int32)
    onehot = gidx[:, None] == jnp.arange(N_GROUPS, dtype=jnp.int32)[None, :]
    cnt = jnp.sum(onehot, axis=0, dtype=jnp.int32)
    ptiles = (cnt + tile - 1) // tile
    tile_end = jnp.cumsum(ptiles)
    off = (tile_end - ptiles) * tile
    pos = jnp.sum(jnp.where(onehot, off[None, :], 0), axis=1) + rank
    tiles = jnp.arange(n_tiles, dtype=jnp.int32)
    tile_group = jnp.minimum(jnp.sum(tiles[:, None] >= tile_end[None, :], axis=1), N_GROUPS - 1)
    return pos.astype(jnp.int32), tile_group.astype(jnp.int32), tile_end[-1:].astype(jnp.int32)


def _sort_kernel(pos_ref, x_ref, xs_in_ref, xs_ref, stage, sem):
    del xs_in_ref
    i = pl.program_id(0)
    last = pl.num_programs(0) - 1
    slot = i & 1
    n = x_ref.shape[0]

    def wait_slot(s):
        pltpu.make_async_copy(stage.at[s], xs_ref.at[pl.ds(0, n)], sem.at[s]).wait()

    pl.when(i >= 2)(lambda: wait_slot(slot))
    stage[slot] = x_ref[...]

    def make_copy(r):
        return pltpu.make_async_copy(_token_tile(stage.at[slot], r), _token_tile(xs_ref, pos_ref[0, 0, r]),
                                     sem.at[slot])

    _start_row_copies(n // SUBLANES, make_copy)

    @pl.when(i == last)
    def _():
        pl.when(i >= 1)(lambda: wait_slot(1 - slot))
        wait_slot(slot)


def _sort_rows(x, pos, n_tokens_out, tm):
    t = pos.shape[0]
    return pl.pallas_call(
        _sort_kernel,
        grid=(t // tm,),
        in_specs=[pl.BlockSpec((1, 1, tm), lambda i: (i, 0, 0), memory_space=pltpu.SMEM),
                  pl.BlockSpec((tm * SUBLANES, LANES), lambda i: (i, 0)),
                  pl.BlockSpec(memory_space=pl.ANY)],
        out_specs=pl.BlockSpec(memory_space=pl.ANY),
        out_shape=jax.ShapeDtypeStruct((n_tokens_out * SUBLANES, LANES), x.dtype),
        input_output_aliases={2: 0},
        scratch_shapes=[pltpu.VMEM((2, tm * SUBLANES, LANES), x.dtype), pltpu.SemaphoreType.DMA((2,))],
        compiler_params=pltpu.CompilerParams(dimension_semantics=("arbitrary",)),
        name="moe_sort_rows",
    )(pos.reshape(t // tm, 1, tm), x, jnp.zeros((n_tokens_out * SUBLANES, LANES), x.dtype))


def _experts_kernel(tg_ref, na_ref, xs_ref, wr_ref, w1_ref, w3_ref, w2_ref, ys_ref, w13b_sc, w2b_sc, *,
                    n_experts):
    i = pl.program_id(0)
    f = w2_ref.shape[2]

    @pl.when((i == 0) | (tg_ref[i] != tg_ref[jnp.maximum(i - 1, 0)]))
    def _():
        for j in range(EXPERTS_PER_GROUP):
            w13b_sc[j, :, 0:f] = w1_ref[0, j].astype(BF16)
            w13b_sc[j, :, f:2 * f] = w3_ref[0, j].astype(BF16)
            w2b_sc[j * f:(j + 1) * f, :] = w2_ref[0, j].astype(BF16)

    @pl.when(i >= na_ref[0])
    def _():
        ys_ref[...] = jnp.zeros_like(ys_ref)

    @pl.when(i < na_ref[0])
    def _():
        x = _from_token_tiles(xs_ref).astype(BF16)
        gates, _ = _router_gates(_dot(x, wr_ref[...]), n_experts)
        lane = lax.broadcasted_iota(jnp.int32, gates.shape, 1)
        first = tg_ref[i] * EXPERTS_PER_GROUP
        hid = []
        for j in range(EXPERTS_PER_GROUP):
            au = _dot(x, w13b_sc[j])
            a = au[:, :f]
            u = au[:, f:]
            gate = jnp.sum(jnp.where(lane == first + j, gates, 0.0), axis=-1, keepdims=True)
            hid.append(((a * jax.nn.sigmoid(a)) * u * gate).astype(BF16))
        _to_token_tiles(ys_ref, _dot(jnp.concatenate(hid, axis=-1), w2b_sc[...]))


def _experts(xs, tile_group, n_active, wr, w1_all, w3_all, w2_all, layer, n_experts, tile):
    f, d = w2_all.shape[2:]
    rows = xs.shape[0] // SUBLANES
    epg = EXPERTS_PER_GROUP
    group_w = lambda i, tg, na: (layer, tg[i], 0, 0)
    grid_spec = pltpu.PrefetchScalarGridSpec(
        num_scalar_prefetch=2,
        grid=(rows // tile,),
        in_specs=[pl.BlockSpec((tile * SUBLANES, LANES), lambda i, tg, na: (i, 0)),
                  pl.BlockSpec((d, LANES), lambda i, tg, na: (0, 0)),
                  pl.BlockSpec((1, epg, d, f), group_w),
                  pl.BlockSpec((1, epg, d, f), group_w),
                  pl.BlockSpec((1, epg, f, d), group_w)],
        out_specs=pl.BlockSpec((tile * SUBLANES, LANES), lambda i, tg, na: (i, 0)),
        scratch_shapes=[pltpu.VMEM((epg, d, 2 * f), BF16), pltpu.VMEM((epg * f, d), BF16)],
    )
    return pl.pallas_call(
        functools.partial(_experts_kernel, n_experts=n_experts),
        grid_spec=grid_spec,
        out_shape=jax.ShapeDtypeStruct(xs.shape, F32),
        compiler_params=pltpu.CompilerParams(dimension_semantics=("arbitrary",)),
        name="moe_experts",
    )(tile_group, n_active, xs, wr, w1_all, w3_all, w2_all)


def _final_kernel(pos_ref, pos_next_ref, h2_ref, ys_ref, g_ref, o_ref, ybuf, sem):
    o_ref[...] = _rms(h2_ref[...] + _gathered_tile(pos_ref, pos_next_ref, ys_ref, ybuf, sem), g_ref[...])


def _final_norm(h2, ys, pos, g, tm):
    t, d = h2.shape
    pos3 = pos.reshape(t // tm, 1, tm)
    return pl.pallas_call(
        _final_kernel,
        grid=(t // tm,),
        in_specs=_pos_specs(t // tm, tm) + [pl.BlockSpec((tm, d), lambda i: (i, 0)),
                                            pl.BlockSpec(memory_space=pl.ANY),
                                            pl.BlockSpec((1, d), lambda i: (0, 0))],
        out_specs=pl.BlockSpec((tm, d), lambda i: (i, 0)),
        out_shape=jax.ShapeDtypeStruct((t, d), F32),
        scratch_shapes=[pltpu.VMEM((2, tm * SUBLANES, LANES), F32), pltpu.SemaphoreType.DMA((2,))],
        compiler_params=pltpu.CompilerParams(dimension_semantics=("arbitrary",)),
        name="final_norm_moe_gather",
    )(pos3, pos3, h2, ys, g)


def kernel(x, mem, norm_mix_g, w_in_a, w_in_b, w_out, conv_w, conv_b, gate_a_w, gate_a_b, gate_x_w, gate_x_b, lru_lambda, diff_lq1, diff_lk1, diff_lq2, diff_lk2, diff_subln_g, rel_bias, mem_norm_g, w_mem_kv, norm_ffn_g, w_group, w_router, w_exp1, w_exp3, w_exp2, final_norm_g):
    batch, seq, d = x.shape
    depth = w_out.shape[0]
    n_experts = w_router.shape[-1]
    t = batch * seq
    rnn_width = conv_w.shape[-1]
    tok_width = d - MEM_WIDTH
    tm = min(512, seq)
    tm_wide = min(1024, seq)
    ts = min(512, seq)
    attn_tile = min(1024, seq)
    moe_tile = min(512, seq)
    moe_tiles = t // moe_tile + N_GROUPS

    kv = _memkv(mem.reshape(-1, d), mem_norm_g.reshape(1, d), w_mem_kv)
    kv = kv.reshape(depth, batch, mem.shape[1], 2 * MEM_WIDTH)
    bias, corner = _bias_tiles(rel_bias, attn_tile)

    h = x.reshape(t, d)
    moe = None
    for i in range(depth):
        j = i // 2
        g_mix = norm_mix_g[i].reshape(1, d)
        if i % 2 == 0:
            outs = _in_proj(h, moe, g_mix, w_in_a, j, kv, i, batch,
                            (rnn_width, rnn_width), (1.0, 1.0), (F32, F32), tm_wide if moe is None else tm)
            if moe is not None:
                h, outs = outs[0], outs[1:]
            ux, ug, mo = outs
            tok = _rglru(ux, ug, conv_w[j], conv_b[j].reshape(1, -1),
                         _block_diag(gate_a_w[j]).astype(BF16), gate_a_b[j].reshape(1, -1),
                         _block_diag(gate_x_w[j]).astype(BF16), gate_x_b[j].reshape(1, -1),
                         lru_lambda[j].reshape(1, -1), batch, ts)
        else:
            outs = _in_proj(h, moe, g_mix, w_in_b, j, kv, i, batch,
                            (tok_width, tok_width, tok_width),
                            (DIFF_QK_DIM ** -0.5 * LOG2E, 1.0, 1.0), (BF16, BF16, BF16), tm)
            if moe is not None:
                h, outs = outs[0], outs[1:]
            q, k, v, mo = outs
            lambda_init = 0.8 - 0.6 * math.exp(-0.3 * i)
            shp = (batch, seq, tok_width)
            tok = _diff_attention(q.reshape(shp), k.reshape(shp), v.reshape(shp), bias, corner,
                                  diff_lq1[j].reshape(1, -1), diff_lk1[j].reshape(1, -1),
                                  diff_lq2[j].reshape(1, -1), diff_lk2[j].reshape(1, -1),
                                  diff_subln_g[j].reshape(1, -1), lambda_init, attn_tile)
            tok = tok.reshape(t, tok_width)
        wr = jnp.concatenate([w_router[i], w_group[i]], axis=-1)
        wr = jnp.pad(wr, ((0, 0), (0, LANES - wr.shape[1]))).astype(BF16)
        h, hn, route = _out_proj(tok, mo, h, w_out, i, norm_ffn_g[i].reshape(1, d), wr, n_experts, tm_wide)
        pos, tile_group, n_active = _route_tables(route, moe_tile, moe_tiles)
        assert d == SUBLANES * LANES
        xs = _sort_rows(hn, pos, moe_tiles * moe_tile, tm_wide)
        ys = _experts(xs, tile_group, n_active, wr, w_exp1, w_exp3, w_exp2, i, n_experts, moe_tile)
        moe = (ys, pos)
    return _final_norm(h, moe[0], moe[1], final_norm_g.reshape(1, d), tm).reshape(batch, seq, d)
```
